```python
import jax, jax.numpy as jnp
from jax import lax
import numpy as np

D_MODEL = 2048
BATCH = 8
SEQ = 2048
DEPTH = 1

D_MIX = D_MODEL
D_CONV = D_MIX // 2
D_ATTN = D_MIX - D_CONV
HEAD_DIM = 64
N_HEADS = D_ATTN // HEAD_DIM
CONV_WIDTH = 3
DILATED_BRANCHES = ((128, 1), (512, 4), (2048, 16))
N_PROJ = 4 * D_CONV + 4 * D_ATTN
EPS = 1e-6
NEG_INF = -1e30

kernel_name = "hybrid_conv_dilated_attn_block"


def rms_norm(x, gain):
    xf = x.astype(jnp.float32)
    y = xf * lax.rsqrt(jnp.mean(xf * xf, axis=-1, keepdims=True) + EPS)
    return (y * gain.astype(jnp.float32)).astype(x.dtype)


def alibi_slopes(n_heads):
    return 2.0 ** (-8.0 * jnp.arange(1, n_heads + 1, dtype=jnp.float32) / n_heads)


def short_conv_centred(u, w, b):
    ch = u.shape[-1]
    pad = CONV_WIDTH // 2
    y = lax.conv_general_dilated(
        u, w[:, None, :].astype(u.dtype), window_strides=(1,), padding=((pad, pad),),
        dimension_numbers=("NWC", "WIO", "NWC"), feature_group_count=ch)
    return y + b.astype(u.dtype)


def dilated_branch(q, k, v, slopes, window, dilation):
    bsz, n_heads, seq, hd = q.shape
    r = dilation
    n = (window // 2) // r
    L = seq // r
    nb = -(-L // n)
    Lp = nb * n

    def to_res(t):
        return t.reshape(bsz, n_heads, L, r, hd).transpose(0, 1, 3, 2, 4)

    qb = jnp.pad(to_res(q), ((0, 0), (0, 0), (0, 0), (0, Lp - L), (0, 0)))
    qb = qb.reshape(bsz, n_heads, r, nb, n, hd)
    kv_pad = ((0, 0), (0, 0), (0, 0), (n, Lp - L + n), (0, 0))
    kb = jnp.pad(to_res(k), kv_pad).reshape(bsz, n_heads, r, nb + 2, n, hd)
    vb = jnp.pad(to_res(v), kv_pad).reshape(bsz, n_heads, r, nb + 2, n, hd)
    shifts = (slice(0, nb), slice(1, nb + 1), slice(2, nb + 2))

    s = jnp.concatenate(
        [jnp.einsum("bhrnqd,bhrnkd->bhrnqk", qb, kb[:, :, :, sl]) for sl in shifts],
        axis=-1).astype(jnp.float32) * (hd ** -0.5)

    qi = jnp.arange(n)
    ki = jnp.arange(3 * n)
    blk = jnp.arange(nb)
    off = ki[None, :] - n - qi[:, None]
    key_idx = blk[:, None] * n + ki[None, :] - n
    valid = (jnp.abs(off) <= n)[None] & ((key_idx >= 0) & (key_idx < L))[:, None, :]
    dist = (r * jnp.abs(off)).astype(jnp.float32)
    bias = -slopes[:, None, None] * dist[None]
    s = jnp.where(valid[None, None, None], s + bias[None, :, None, None], NEG_INF)

    m = jnp.max(s, axis=-1, keepdims=True)
    p = jnp.exp(s - m)
    den = jnp.sum(p, axis=-1, keepdims=True)
    o = jnp.einsum("bhrnqk,bhrnkd->bhrnqd", p[..., 0:n], vb[:, :, :, shifts[0]].astype(jnp.float32))
    o = o + jnp.einsum("bhrnqk,bhrnkd->bhrnqd", p[..., n:2 * n], vb[:, :, :, shifts[1]].astype(jnp.float32))
    o = o + jnp.einsum("bhrnqk,bhrnkd->bhrnqd", p[..., 2 * n:], vb[:, :, :, shifts[2]].astype(jnp.float32))
    o = o / den
    lse = (m + jnp.log(den))[..., 0]

    o = o.reshape(bsz, n_heads, r, Lp, hd)[:, :, :, :L]
    o = o.transpose(0, 1, 3, 2, 4).reshape(bsz, n_heads, seq, hd)
    lse = lse.reshape(bsz, n_heads, r, Lp)[..., :L]
    lse = lse.transpose(0, 1, 3, 2).reshape(bsz, n_heads, seq)
    return o, lse


def dilated_mixture_attention(q, k, v):
    slopes = alibi_slopes(q.shape[1])
    outs, lses = [], []
    for window, dilation in DILATED_BRANCHES:
        o, lse = dilated_branch(q, k, v, slopes, window, dilation)
        outs.append(o)
        lses.append(lse)
    alpha = jax.nn.softmax(jnp.stack(lses, axis=0), axis=0)
    return jnp.sum(alpha[..., None] * jnp.stack(outs, axis=0), axis=0)


def hybrid_layer(x, mod, g_pre, w_in, conv_w, conv_b, g_conv, g_attn, w_out, g_post):
    bsz, seq, _ = x.shape
    shift, scale, gate = jnp.split(mod, 3, axis=-1)
    h = rms_norm(x, g_pre) * (1.0 + scale[:, None, :]) + shift[:, None, :]

    proj = jnp.einsum("bsd,dn->bsn", h, w_in)
    cuts = np.cumsum([D_CONV, D_CONV, D_CONV, D_CONV, D_ATTN, D_ATTN, D_ATTN])
    u, b_gate, c_gate, z_c, q, k, v, z_a = jnp.split(proj, cuts, axis=-1)

    y_c = b_gate * short_conv_centred(c_gate * u, conv_w, conv_b)
    y_c = rms_norm(y_c, g_conv) * jax.nn.silu(z_c)

    def heads(t):
        return t.reshape(bsz, seq, N_HEADS, HEAD_DIM).transpose(0, 2, 1, 3)
    o = dilated_mixture_attention(heads(q), heads(k), heads(v))
    y_a = o.transpose(0, 2, 1, 3).reshape(bsz, seq, D_ATTN).astype(x.dtype)
    y_a = rms_norm(y_a, g_attn) * jax.nn.silu(z_a)

    y = jnp.einsum("bsn,nd->bsd", jnp.concatenate([y_c, y_a], axis=-1), w_out)
    return x + gate[:, None, :] * rms_norm(y, g_post)


def setup_inputs(seed: int = 0) -> dict:
    key = jax.random.key(seed)
    ks = jax.random.split(key, 14)
    f32 = jnp.float32
    x = jax.random.normal(ks[0], (BATCH, SEQ, D_MODEL), f32)
    c = jax.random.normal(ks[1], (BATCH, D_MODEL), f32)
    w_ada = jax.random.normal(ks[2], (DEPTH, D_MODEL, 3 * D_MODEL), f32) * D_MODEL ** -0.5
    b_ada = 0.02 * jax.random.normal(ks[3], (DEPTH, 3 * D_MODEL), f32)
    g_pre = 1.0 + 0.05 * jax.random.normal(ks[4], (DEPTH, D_MODEL), f32)
    w_in = jax.random.normal(ks[5], (DEPTH, D_MODEL, N_PROJ), f32) * D_MODEL ** -0.5
    conv_w = jax.random.normal(ks[6], (DEPTH, CONV_WIDTH, D_CONV), f32) * CONV_WIDTH ** -0.5
    conv_b = 0.02 * jax.random.normal(ks[7], (DEPTH, D_CONV), f32)
    g_conv = 1.0 + 0.05 * jax.random.normal(ks[8], (DEPTH, D_CONV), f32)
    g_attn = 1.0 + 0.05 * jax.random.normal(ks[9], (DEPTH, D_ATTN), f32)
    w_out = jax.random.normal(ks[10], (DEPTH, D_MIX, D_MODEL), f32) * D_MIX ** -0.5
    g_post = 1.0 + 0.05 * jax.random.normal(ks[11], (DEPTH, D_MODEL), f32)
    return {"x": x, "c": c, "w_ada": w_ada, "b_ada": b_ada, "g_pre": g_pre,
            "w_in": w_in, "conv_w": conv_w, "conv_b": conv_b, "g_conv": g_conv,
            "g_attn": g_attn, "w_out": w_out, "g_post": g_post}


def reference(x, c, w_ada, b_ada, g_pre, w_in, conv_w, conv_b, g_conv, g_attn, w_out, g_post):
    c_act = jax.nn.silu(c)
    for layer in range(DEPTH):
        mod = jnp.einsum("bd,dn->bn", c_act, w_ada[layer]) + b_ada[layer]
        x = hybrid_layer(x, mod, g_pre[layer], w_in[layer], conv_w[layer], conv_b[layer],
                         g_conv[layer], g_attn[layer], w_out[layer], g_post[layer])
    return x
```

```python
import functools

import jax
import jax.numpy as jnp
from jax import lax
from jax.experimental import pallas as pl
from jax.experimental.pallas import tpu as pltpu

HEAD_DIM = 64
CONV_WIDTH = 3
DILATED_BRANCHES = ((128, 1), (512, 4), (2048, 16))
EPS = 1e-6
NEG_INF = -1e30

LANES = 128
HEADS_PER_LANE_GROUP = LANES // HEAD_DIM
BF16_SUBLANES = 16
V7X_VMEM_LIMIT = 56 * 1024 * 1024

F32 = jnp.float32
BF16 = jnp.bfloat16


def _silu(v):
    return v / (1.0 + jnp.exp(-v))


def _params(semantics, vmem_bytes):
    return pltpu.CompilerParams(dimension_semantics=semantics,
                                vmem_limit_bytes=min(int(vmem_bytes), V7X_VMEM_LIMIT))


def _mod_kernel(c_ref, w_ref, b_ref, o_ref):
    c_act = _silu(c_ref[...]).astype(BF16)
    w = w_ref[...].astype(BF16)
    o_ref[...] = jnp.dot(c_act, w, preferred_element_type=F32) + b_ref[...]


def _modulation(c, w_ada, b_ada):
    bsz, d = c.shape
    n = w_ada.shape[1]
    tn = 512
    return pl.pallas_call(
        _mod_kernel,
        grid=(n // tn,),
        in_specs=[pl.BlockSpec((bsz, d), lambda j: (0, 0)),
                  pl.BlockSpec((d, tn), lambda j: (0, j)),
                  pl.BlockSpec((1, tn), lambda j: (0, j))],
        out_specs=pl.BlockSpec((bsz, tn), lambda j: (0, j)),
        out_shape=jax.ShapeDtypeStruct((bsz, n), F32),
        compiler_params=_params(("arbitrary",), 4 * d * tn * 4),
        name="adaln_mod",
    )(c, w_ada, b_ada.reshape(1, n))


def _inproj_kernel(x_ref, shift_ref, scale_ref, g_ref, w_ref, o_ref, h_ref, *, q_block, q_scale, chunk):
    j = pl.program_id(1)

    @pl.when(j == 0)
    def _():
        g = g_ref[...]
        one_plus_scale = 1.0 + scale_ref[...]
        shift = shift_ref[...]

        def body(ci, carry):
            r0 = pl.multiple_of(ci * chunk, chunk)
            xc = x_ref[pl.ds(r0, chunk), :]
            ms = jnp.mean(xc * xc, axis=-1, keepdims=True)
            y = xc * lax.rsqrt(ms + EPS) * g
            h_ref[pl.ds(r0, chunk), :] = (y * one_plus_scale + shift).astype(BF16)
            return carry

        lax.fori_loop(0, x_ref.shape[0] // chunk, body, 0)

    acc = jnp.dot(h_ref[...], w_ref[...], preferred_element_type=F32)
    fac = jnp.where(j == q_block, q_scale, 1.0).astype(F32)
    o_ref[...] = (acc * fac).astype(o_ref.dtype)


def _in_projection(x2d, mod3, g_pre, w_in_bf16, seq, q_col):
    m, d = x2d.shape
    n = w_in_bf16.shape[1]
    tm, tn = 1024, 1024
    tiles_per_seq = seq // tm
    kern = functools.partial(_inproj_kernel, q_block=q_col // tn, q_scale=HEAD_DIM ** -0.5, chunk=128)
    vmem = 2 * tm * d * 4 + 2 * d * tn * 2 + 2 * tm * tn * 2 + tm * d * 2 + (8 << 20)
    return pl.pallas_call(
        kern,
        grid=(m // tm, n // tn),
        in_specs=[pl.BlockSpec((tm, d), lambda i, j: (i, 0)),
                  pl.BlockSpec((None, 1, d), lambda i, j: (i // tiles_per_seq, 0, 0)),
                  pl.BlockSpec((None, 1, d), lambda i, j: (i // tiles_per_seq, 0, 1)),
                  pl.BlockSpec((1, d), lambda i, j: (0, 0)),
                  pl.BlockSpec((d, tn), lambda i, j: (0, j))],
        out_specs=pl.BlockSpec((tm, tn), lambda i, j: (i, j)),
        out_shape=jax.ShapeDtypeStruct((m, n), BF16),
        scratch_shapes=[pltpu.VMEM((tm, d), BF16)],
        compiler_params=_params(("arbitrary", "arbitrary"), vmem),
        name="in_proj",
    )(x2d, mod3, mod3, g_pre.reshape(1, d), w_in_bf16)


def _attn_kernel(slopes_ref, q_ref, k_ref, v_ref, o_ref, lse_ref, bias_ref, *,
                 seq_len, dilation, half_window, tq, tk):
    bb, lq, width = q_ref.shape
    n_pairs = width // LANES
    n_qt = lq // tq
    n_var = bias_ref.shape[0]
    max_ks = seq_len - tk
    first = (pl.program_id(0) == 0) & (pl.program_id(1) == 0) & (pl.program_id(2) == 0)

    @pl.when(first)
    def _():
        qi = lax.broadcasted_iota(jnp.int32, (tq, tk), 0)
        ki = lax.broadcasted_iota(jnp.int32, (tq, tk), 1)
        for var in range(n_var):
            delta = (0, half_window, tk - tq)[var]
            off = jnp.abs(ki - qi - delta)
            valid = off <= half_window
            dist = (dilation * off).astype(F32)
            for h in range(2 * n_pairs):
                bias = jnp.where(valid, -slopes_ref[h] * dist, NEG_INF)
                r0 = (h % 2) * tq
                bias_ref[var, h // 2, r0:r0 + tq, :] = bias

    lane = lax.broadcasted_iota(jnp.int32, (1, LANES), 1)
    even_head = lane < HEAD_DIM
    keep_even = even_head.astype(BF16)
    keep_odd = 1.0 - keep_even
    q_tile0 = pl.program_id(2) * n_qt

    def tile(b, jt):
        qs_local = pl.multiple_of(jt * tq, tq)
        qs = (q_tile0 + jt) * tq
        ks = pl.multiple_of(jnp.clip(qs - half_window, 0, max_ks), half_window)
        if n_var == 1:
            var = 0
        else:
            var = jnp.where(qs == 0, 0, jnp.where(qs == seq_len - tq, 2, 1))
        lse_tile = jnp.zeros((tq, LANES), F32)
        for p in range(n_pairs):
            cols = slice(p * LANES, (p + 1) * LANES)
            q2 = q_ref[b, pl.ds(qs_local, tq), cols]
            kw = k_ref[b, pl.ds(ks, tk), cols]
            vw = v_ref[b, pl.ds(ks, tk), cols]
            qq = jnp.concatenate([q2 * keep_even, q2 * keep_odd], axis=0)
            s = lax.dot_general(qq, kw, (((1,), (1,)), ((), ())), preferred_element_type=F32)
            s = s + bias_ref[var, p]
            m = jnp.max(s, axis=-1, keepdims=True)
            e = jnp.exp(s - m)
            den = jnp.sum(e, axis=-1, keepdims=True)
            pv = jnp.dot(e.astype(BF16), vw, preferred_element_type=F32)
            pv = pv * (1.0 / den)
            o_ref[b, pl.ds(qs_local, tq), cols] = jnp.where(even_head, pv[:tq], pv[tq:])
            lse = m + jnp.log(den)
            lse_tile = jnp.where(lane == 2 * p, lse[:tq], lse_tile)
            lse_tile = jnp.where(lane == 2 * p + 1, lse[tq:], lse_tile)
        lse_ref[b, pl.ds(qs_local, tq), :] = lse_tile

    def per_batch(b, carry):
        if n_qt == 1:
            tile(b, 0)
        else:
            lax.fori_loop(0, n_qt, lambda jt, c: (tile(b, jt), c)[1], 0)
        return carry

    lax.fori_loop(0, bb, per_batch, 0)


def _attention_branch(proj, slopes, bsz, seq, n_proj, q_col, d_attn, window, dilation):
    r = dilation
    half_window = (window // 2) // r
    seq_len = seq // r
    tq = 2 * half_window
    tk = min(2 * tq, seq_len)
    lq = min(seq_len, 512)
    bb = max(1, min(bsz, 1024 // seq_len))
    n_var = 1 if tk == seq_len else 3
    n_pairs = d_attn // LANES
    blocks_per_row = n_proj // d_attn
    qb = q_col // d_attn

    proj_r = proj.reshape(bsz, seq_len, r * n_proj)
    kern = functools.partial(_attn_kernel, seq_len=seq_len, dilation=r, half_window=half_window,
                             tq=tq, tk=tk)
    vmem = (2 * bb * (lq + 2 * seq_len) * d_attn * 2 + 2 * bb * lq * (d_attn + LANES) * 4
            + n_var * n_pairs * 2 * tq * tk * 4 + (12 << 20))
    o, lse = pl.pallas_call(
        kern,
        grid=(bsz // bb, r, seq_len // lq),
        in_specs=[pl.BlockSpec(memory_space=pltpu.SMEM),
                  pl.BlockSpec((bb, lq, d_attn), lambda b, s, t: (b, t, s * blocks_per_row + qb)),
                  pl.BlockSpec((bb, seq_len, d_attn), lambda b, s, t: (b, 0, s * blocks_per_row + qb + 1)),
                  pl.BlockSpec((bb, seq_len, d_attn), lambda b, s, t: (b, 0, s * blocks_per_row + qb + 2))],
        out_specs=[pl.BlockSpec((bb, lq, d_attn), lambda b, s, t: (b, t, s)),
                   pl.BlockSpec((bb, lq, LANES), lambda b, s, t: (b, t, s))],
        out_shape=[jax.ShapeDtypeStruct((bsz, seq_len, r * d_attn), F32),
                   jax.ShapeDtypeStruct((bsz, seq_len, r * LANES), F32)],
        scratch_shapes=[pltpu.VMEM((n_var, n_pairs, 2 * tq, tk), F32)],
        compiler_params=_params(("arbitrary", "arbitrary", "arbitrary"), vmem),
        name=f"dilated_attn_r{r}",
    )(slopes, proj_r, proj_r, proj_r)
    return o.reshape(bsz * seq, d_attn), lse.reshape(bsz * seq, LANES)


def _mix_kernel(u_ref, bg_ref, cg_ref, zc_ref, za_ref, up_ref, cp_ref, un_ref, cn_ref,
                o1_ref, o2_ref, o3_ref, l1_ref, l2_ref, l3_ref,
                cw_ref, cb_ref, gc_ref, ga_ref, y_ref, om_ref, *, tiles_per_seq):
    tr, dc = u_ref.shape
    da = o1_ref.shape[1]
    i = pl.program_id(0)
    t_in_seq = i % tiles_per_seq

    v = cg_ref[...].astype(F32) * u_ref[...].astype(F32)
    last = BF16_SUBLANES - 1
    v_before = cp_ref[last:last + 1, :].astype(F32) * up_ref[last:last + 1, :].astype(F32)
    v_after = cn_ref[0:1, :].astype(F32) * un_ref[0:1, :].astype(F32)
    v_before = jnp.where(t_in_seq == 0, 0.0, v_before)
    v_after = jnp.where(t_in_seq == tiles_per_seq - 1, 0.0, v_after)
    row = lax.broadcasted_iota(jnp.int32, (tr, 1), 0)
    v_prev = jnp.where(row == 0, v_before, pltpu.roll(v, 1, axis=0))
    v_next = jnp.where(row == tr - 1, v_after, pltpu.roll(v, tr - 1, axis=0))
    cw = cw_ref[...]
    conv = cw[0:1, :] * v_prev + cw[1:2, :] * v + cw[2:3, :] * v_next + cb_ref[...]
    yc = bg_ref[...].astype(F32) * conv
    yc = yc * lax.rsqrt(jnp.mean(yc * yc, axis=-1, keepdims=True) + EPS) * gc_ref[...]
    y_ref[:, 0:dc] = (yc * _silu(zc_ref[...].astype(F32))).astype(y_ref.dtype)

    l1, l2, l3 = l1_ref[...], l2_ref[...], l3_ref[...]
    lmax = jnp.maximum(jnp.maximum(l1, l2), l3)
    e1, e2, e3 = jnp.exp(l1 - lmax), jnp.exp(l2 - lmax), jnp.exp(l3 - lmax)
    inv = 1.0 / (e1 + e2 + e3)
    a1, a2, a3 = e1 * inv, e2 * inv, e3 * inv
    lane = lax.broadcasted_iota(jnp.int32, (1, LANES), 1)
    even_head = lane < HEAD_DIM
    ssq = jnp.zeros((tr, 1), F32)
    for p in range(da // LANES):
        cols = slice(p * LANES, (p + 1) * LANES)
        h0, h1 = 2 * p, 2 * p + 1
        w1 = jnp.where(even_head, a1[:, h0:h0 + 1], a1[:, h1:h1 + 1])
        w2 = jnp.where(even_head, a2[:, h0:h0 + 1], a2[:, h1:h1 + 1])
        w3 = jnp.where(even_head, a3[:, h0:h0 + 1], a3[:, h1:h1 + 1])
        om = w1 * o1_ref[:, cols] + w2 * o2_ref[:, cols] + w3 * o3_ref[:, cols]
        om_ref[:, cols] = om
        ssq = ssq + jnp.sum(om * om, axis=-1, keepdims=True)
    rstd = lax.rsqrt(ssq * (1.0 / da) + EPS)
    ya = om_ref[...] * rstd * ga_ref[...]
    y_ref[:, dc:dc + da] = (ya * _silu(za_ref[...].astype(F32))).astype(y_ref.dtype)


def _mix(proj, outs, lses, conv_w, conv_b, g_conv, g_attn, seq, d_conv, d_attn):
    m = proj.shape[0]
    tr = 256
    halo = BF16_SUBLANES
    hb = tr // halo
    n_halo_blocks = m // halo
    u_c, b_c, c_c, zc_c = 0, 1, 2, 3
    za_c = (4 * d_conv + 3 * d_attn) // d_attn

    def main(col):
        return pl.BlockSpec((tr, d_conv), lambda i: (i, col))

    def before(col):
        return pl.BlockSpec((halo, d_conv), lambda i: (jnp.maximum(i * hb - 1, 0), col))

    def after(col):
        return pl.BlockSpec((halo, d_conv), lambda i: (jnp.minimum((i + 1) * hb, n_halo_blocks - 1), col))

    def row_vec(n, rows=1):
        return pl.BlockSpec((rows, n), lambda i: (0, 0))

    o_spec = pl.BlockSpec((tr, d_attn), lambda i: (i, 0))
    l_spec = pl.BlockSpec((tr, LANES), lambda i: (i, 0))
    kern = functools.partial(_mix_kernel, tiles_per_seq=seq // tr)
    vmem = 2 * (5 * tr * d_conv * 2 + 3 * tr * d_attn * 4 + tr * (d_conv + d_attn) * 2) + (16 << 20)
    return pl.pallas_call(
        kern,
        grid=(m // tr,),
        in_specs=[main(u_c), main(b_c), main(c_c), main(zc_c),
                  pl.BlockSpec((tr, d_attn), lambda i: (i, za_c)),
                  before(u_c), before(c_c), after(u_c), after(c_c),
                  o_spec, o_spec, o_spec, l_spec, l_spec, l_spec,
                  row_vec(d_conv, CONV_WIDTH), row_vec(d_conv), row_vec(d_conv), row_vec(d_attn)],
        out_specs=pl.BlockSpec((tr, d_conv + d_attn), lambda i: (i, 0)),
        out_shape=jax.ShapeDtypeStruct((m, d_conv + d_attn), BF16),
        scratch_shapes=[pltpu.VMEM((tr, d_attn), F32)],
        compiler_params=_params(("arbitrary",), vmem),
        name="mix_groups",
    )(proj, proj, proj, proj, proj, proj, proj, proj, proj,
      outs[0], outs[1], outs[2], lses[0], lses[1], lses[2],
      conv_w, conv_b.reshape(1, d_conv), g_conv.reshape(1, d_conv), g_attn.reshape(1, d_attn))


def _outproj_kernel(y_ref, w_ref, x_ref, gate_ref, g_ref, o_ref):
    y = jnp.dot(y_ref[...], w_ref[...], preferred_element_type=F32)
    yn = y * lax.rsqrt(jnp.mean(y * y, axis=-1, keepdims=True) + EPS) * g_ref[...]
    o_ref[...] = x_ref[...] + gate_ref[...] * yn


def _out_projection(y, w_out_bf16, x2d, mod3, g_post, seq):
    m, d = x2d.shape
    k = y.shape[1]
    tm = 512
    tiles_per_seq = seq // tm
    vmem = 2 * (tm * k * 2 + k * d * 2 + 2 * tm * d * 4) + tm * d * 4 + (8 << 20)
    return pl.pallas_call(
        _outproj_kernel,
        grid=(m // tm,),
        in_specs=[pl.BlockSpec((tm, k), lambda i: (i, 0)),
                  pl.BlockSpec((k, d), lambda i: (0, 0)),
                  pl.BlockSpec((tm, d), lambda i: (i, 0)),
                  pl.BlockSpec((None, 1, d), lambda i: (i // tiles_per_seq, 0, 2)),
                  pl.BlockSpec((1, d), lambda i: (0, 0))],
        out_specs=pl.BlockSpec((tm, d), lambda i: (i, 0)),
        out_shape=jax.ShapeDtypeStruct((m, d), F32),
        compiler_params=_params(("arbitrary",), vmem),
        name="out_proj",
    )(y, w_out_bf16, x2d, mod3, g_post.reshape(1, d))


def _layer(x, mod, g_pre, w_in, conv_w, conv_b, g_conv, g_attn, w_out, g_post):
    bsz, seq, d = x.shape
    d_conv = conv_w.shape[1]
    d_attn = g_attn.shape[0]
    n_proj = w_in.shape[1]
    n_heads = d_attn // HEAD_DIM
    q_col = 4 * d_conv
    assert d_attn % LANES == 0 and d_conv == d_attn and n_proj == 4 * d_conv + 4 * d_attn

    x2d = x.reshape(bsz * seq, d)
    mod3 = mod.reshape(bsz, 1, 3 * d)
    proj = _in_projection(x2d, mod3, g_pre, w_in.astype(BF16), seq, q_col)

    slopes = 2.0 ** (-8.0 * jnp.arange(1, n_heads + 1, dtype=F32) / n_heads)
    outs, lses = [], []
    for window, dilation in DILATED_BRANCHES:
        o, lse = _attention_branch(proj, slopes, bsz, seq, n_proj, q_col, d_attn, window, dilation)
        outs.append(o)
        lses.append(lse)

    y = _mix(proj, outs, lses, conv_w, conv_b, g_conv, g_attn, seq, d_conv, d_attn)
    out = _out_projection(y, w_out.astype(BF16), x2d, mod3, g_post, seq)
    return out.reshape(bsz, seq, d)


def kernel(x, c, w_ada, b_ada, g_pre, w_in, conv_w, conv_b, g_conv, g_attn, w_out, g_post):
    for layer in range(w_ada.shape[0]):
        mod = _modulation(c, w_ada[layer], b_ada[layer])
        x = _layer(x, mod, g_pre[layer], w_in[layer], conv_w[layer], conv_b[layer],
                   g_conv[layer], g_attn[layer], w_out[layer], g_post[layer])
    return x
```

```python
import functools

import jax
import jax.numpy as jnp
from jax import lax
from jax.experimental import pallas as pl
from jax.experimental.pallas import tpu as pltpu

HEAD_DIM = 64
CONV_WIDTH = 3
DILATED_BRANCHES = ((128, 1), (512, 4), (2048, 16))
EPS = 1e-6
NEG_INF = -1e30

LANES = 128
BF16_SUBLANES = 16
V7X_VMEM_LIMIT = 56 * 1024 * 1024

F32 = jnp.float32
BF16 = jnp.bfloat16


def _silu(v):
    return v / (1.0 + jnp.exp(-v))


def _params(semantics, vmem_bytes):
    return pltpu.CompilerParams(dimension_semantics=semantics,
                                vmem_limit_bytes=min(int(vmem_bytes), V7X_VMEM_LIMIT))


def _mod_kernel(c_ref, w_ref, b_ref, o_ref):
    c_act = _silu(c_ref[...]).astype(BF16)
    w = w_ref[...].astype(BF16)
    o_ref[...] = jnp.dot(c_act, w, preferred_element_type=F32) + b_ref[...]


def _modulation(c, w_ada, b_ada):
    bsz, d = c.shape
    n = w_ada.shape[1]
    tn = 512
    return pl.pallas_call(
        _mod_kernel,
        grid=(n // tn,),
        in_specs=[pl.BlockSpec((bsz, d), lambda j: (0, 0)),
                  pl.BlockSpec((d, tn), lambda j: (0, j)),
                  pl.BlockSpec((1, tn), lambda j: (0, j))],
        out_specs=pl.BlockSpec((bsz, tn), lambda j: (0, j)),
        out_shape=jax.ShapeDtypeStruct((bsz, n), F32),
        compiler_params=_params(("arbitrary",), 4 * d * tn * 4),
        name="adaln_mod",
    )(c, w_ada, b_ada.reshape(1, n))


def _inproj_kernel(x_ref, shift_ref, scale_ref, g_ref, w_ref, o_ref, *rest,
                   dilations, q_block, q_scale, chunk):
    res_refs, (h_ref, acc_ref) = rest[:len(dilations)], rest[len(dilations):]
    tm, tn = o_ref.shape
    j = pl.program_id(1)

    @pl.when(j == 0)
    def _():
        g = g_ref[...]
        one_plus_scale = 1.0 + scale_ref[...]
        shift = shift_ref[...]

        def body(ci, carry):
            r0 = pl.multiple_of(ci * chunk, chunk)
            xc = x_ref[pl.ds(r0, chunk), :]
            ms = jnp.mean(xc * xc, axis=-1, keepdims=True)
            y = xc * lax.rsqrt(ms + EPS) * g
            h_ref[pl.ds(r0, chunk), :] = (y * one_plus_scale + shift).astype(BF16)
            return carry

        lax.fori_loop(0, tm // chunk, body, 0)

    acc = jnp.dot(h_ref[...], w_ref[...], preferred_element_type=F32)
    is_qkv = (j >= q_block) & (j < q_block + 3)

    @pl.when(jnp.logical_not(is_qkv))
    def _():
        o_ref[...] = acc.astype(o_ref.dtype)

    @pl.when(is_qkv)
    def _():
        fac = jnp.where(j == q_block, q_scale, 1.0).astype(F32)
        scaled = acc * fac
        o_ref[...] = scaled.astype(o_ref.dtype)
        for c in range(tn // LANES):
            acc_ref[c] = scaled[:, c * LANES:(c + 1) * LANES]
        for r, res_ref in zip(dilations, res_refs):
            for res in range(r):
                for c in range(tn // LANES):
                    rows = acc_ref[c, pl.ds(res, tm // r, stride=r), :]
                    res_ref[res, :, c * LANES:(c + 1) * LANES] = rows.astype(res_ref.dtype)


def _in_projection(x2d, mod3, g_pre, w_in_bf16, bsz, seq, q_col, d_attn, dilations):
    m, d = x2d.shape
    n = w_in_bf16.shape[1]
    tm, tn = 1024, d_attn
    tps = seq // tm
    q_block = q_col // tn
    kern = functools.partial(_inproj_kernel, dilations=dilations, q_block=q_block,
                             q_scale=HEAD_DIM ** -0.5, chunk=128)

    def res_spec(r):
        return pl.BlockSpec((None, r, tm // r, tn),
                            lambda i, j: (i // tps, 0, i % tps, jnp.clip(j - q_block, 0, 2)))

    vmem = (2 * tm * d * 4 + 2 * d * tn * 2 + 2 * (1 + len(dilations)) * tm * tn * 2
            + tm * d * 2 + tm * tn * 4 + (10 << 20))
    return pl.pallas_call(
        kern,
        grid=(m // tm, n // tn),
        in_specs=[pl.BlockSpec((tm, d), lambda i, j: (i, 0)),
                  pl.BlockSpec((None, 1, d), lambda i, j: (i // tps, 0, 0)),
                  pl.BlockSpec((None, 1, d), lambda i, j: (i // tps, 0, 1)),
                  pl.BlockSpec((1, d), lambda i, j: (0, 0)),
                  pl.BlockSpec((d, tn), lambda i, j: (0, j))],
        out_specs=[pl.BlockSpec((tm, tn), lambda i, j: (i, j))] + [res_spec(r) for r in dilations],
        out_shape=[jax.ShapeDtypeStruct((m, n), BF16)]
        + [jax.ShapeDtypeStruct((bsz, r, seq // r, 3 * d_attn), BF16) for r in dilations],
        scratch_shapes=[pltpu.VMEM((tm, d), BF16), pltpu.VMEM((tn // LANES, tm, LANES), F32)],
        compiler_params=_params(("arbitrary", "arbitrary"), vmem),
        name="in_proj",
    )(x2d, mod3, mod3, g_pre.reshape(1, d), w_in_bf16)


def _attn_kernel(slopes_ref, q_ref, k_ref, v_ref, o_ref, lse_ref, bias_ref, *,
                 seq_len, dilation, half_window, tq, tk):
    bb, lq, width = q_ref.shape
    n_pairs = width // LANES
    n_qt = lq // tq
    n_var = bias_ref.shape[0]
    max_ks = seq_len - tk
    first = (pl.program_id(0) == 0) & (pl.program_id(1) == 0)

    @pl.when(first)
    def _():
        qi = lax.broadcasted_iota(jnp.int32, (tq, tk), 0)
        ki = lax.broadcasted_iota(jnp.int32, (tq, tk), 1)
        for var in range(n_var):
            delta = (0, half_window, tk - tq)[var]
            off = jnp.abs(ki - qi - delta)
            valid = off <= half_window
            dist = (dilation * off).astype(F32)
            for h in range(2 * n_pairs):
                bias = jnp.where(valid, -slopes_ref[h] * dist, NEG_INF)
                r0 = (h % 2) * tq
                bias_ref[var, h // 2, r0:r0 + tq, :] = bias

    lane = lax.broadcasted_iota(jnp.int32, (1, LANES), 1)
    even_head = lane < HEAD_DIM
    keep_even = even_head.astype(BF16)
    keep_odd = 1.0 - keep_even
    q_tile0 = pl.program_id(1) * n_qt

    def tile(b, jt):
        qs_local = pl.multiple_of(jt * tq, tq)
        qs = (q_tile0 + jt) * tq
        ks = pl.multiple_of(jnp.clip(qs - half_window, 0, max_ks), half_window)
        if n_var == 1:
            var = 0
        else:
            var = jnp.where(qs == 0, 0, jnp.where(qs == seq_len - tq, 2, 1))
        lse_tile = jnp.zeros((tq, LANES), F32)
        for p in range(n_pairs):
            cols = slice(p * LANES, (p + 1) * LANES)
            q2 = q_ref[b, pl.ds(qs_local, tq), cols]
            kw = k_ref[b, pl.ds(ks, tk), cols]
            vw = v_ref[b, pl.ds(ks, tk), cols]
            qq = jnp.concatenate([q2 * keep_even, q2 * keep_odd], axis=0)
            s = lax.dot_general(qq, kw, (((1,), (1,)), ((), ())), preferred_element_type=F32)
            s = s + bias_ref[var, p]
            m = jnp.max(s, axis=-1, keepdims=True)
            e = jnp.exp(s - m)
            den = jnp.sum(e, axis=-1, keepdims=True)
            pv = jnp.dot(e.astype(BF16), vw, preferred_element_type=F32)
            pv = pv * (1.0 / den)
            o_ref[b, pl.ds(qs_local, tq), cols] = jnp.where(even_head, pv[:tq], pv[tq:])
            lse = m + jnp.log(den)
            lse_tile = jnp.where(lane == 2 * p, lse[:tq], lse_tile)
            lse_tile = jnp.where(lane == 2 * p + 1, lse[tq:], lse_tile)
        lse_ref[b, pl.ds(qs_local, tq), :] = lse_tile

    def per_batch(b, carry):
        if n_qt == 1:
            tile(b, 0)
        else:
            lax.fori_loop(0, n_qt, lambda jt, c: (tile(b, jt), c)[1], 0)
        return carry

    lax.fori_loop(0, bb, per_batch, 0)


def _attention_branch(qkv, slopes, q_block, d_attn, window, dilation):
    n_seq, seq_len, _ = qkv.shape
    half_window = (window // 2) // dilation
    tq = 2 * half_window
    tk = min(2 * tq, seq_len)
    lq = min(seq_len, 512)
    bb = max(1, min(n_seq, 1024 // seq_len))
    n_var = 1 if tk == seq_len else 3
    n_pairs = d_attn // LANES

    kern = functools.partial(_attn_kernel, seq_len=seq_len, dilation=dilation, half_window=half_window,
                             tq=tq, tk=tk)
    vmem = (2 * bb * (lq + 2 * seq_len) * d_attn * 2 + 2 * bb * lq * (d_attn + LANES) * 4
            + n_var * n_pairs * 2 * tq * tk * 4 + (12 << 20))
    return pl.pallas_call(
        kern,
        grid=(n_seq // bb, seq_len // lq),
        in_specs=[pl.BlockSpec(memory_space=pltpu.SMEM),
                  pl.BlockSpec((bb, lq, d_attn), lambda b, t: (b, t, q_block)),
                  pl.BlockSpec((bb, seq_len, d_attn), lambda b, t: (b, 0, q_block + 1)),
                  pl.BlockSpec((bb, seq_len, d_attn), lambda b, t: (b, 0, q_block + 2))],
        out_specs=[pl.BlockSpec((bb, lq, d_attn), lambda b, t: (b, t, 0)),
                   pl.BlockSpec((bb, lq, LANES), lambda b, t: (b, t, 0))],
        out_shape=[jax.ShapeDtypeStruct((n_seq, seq_len, d_attn), F32),
                   jax.ShapeDtypeStruct((n_seq, seq_len, LANES), F32)],
        scratch_shapes=[pltpu.VMEM((n_var, n_pairs, 2 * tq, tk), F32)],
        compiler_params=_params(("arbitrary", "arbitrary"), vmem),
        name=f"dilated_attn_r{dilation}",
    )(slopes, qkv, qkv, qkv)


def _mix_kernel(u_ref, bg_ref, cg_ref, zc_ref, za_ref, up_ref, cp_ref, un_ref, cn_ref,
                o1_ref, l1_ref, cw_ref, cb_ref, gc_ref, ga_ref, *rest, tiles_per_seq, dilations):
    nd = len(dilations)
    od_refs, ld_refs = rest[:nd], rest[nd:2 * nd]
    y_ref, om_ref = rest[2 * nd], rest[2 * nd + 1]
    onat_refs, lnat_refs = rest[2 * nd + 2:3 * nd + 2], rest[3 * nd + 2:]
    tr, dc = u_ref.shape
    da = o1_ref.shape[1]
    n_pairs = da // LANES
    i = pl.program_id(0)
    t_in_seq = i % tiles_per_seq

    v = cg_ref[...].astype(F32) * u_ref[...].astype(F32)
    last = BF16_SUBLANES - 1
    v_before = cp_ref[last:last + 1, :].astype(F32) * up_ref[last:last + 1, :].astype(F32)
    v_after = cn_ref[0:1, :].astype(F32) * un_ref[0:1, :].astype(F32)
    v_before = jnp.where(t_in_seq == 0, 0.0, v_before)
    v_after = jnp.where(t_in_seq == tiles_per_seq - 1, 0.0, v_after)
    row = lax.broadcasted_iota(jnp.int32, (tr, 1), 0)
    v_prev = jnp.where(row == 0, v_before, pltpu.roll(v, 1, axis=0))
    v_next = jnp.where(row == tr - 1, v_after, pltpu.roll(v, tr - 1, axis=0))
    cw = cw_ref[...]
    conv = cw[0:1, :] * v_prev + cw[1:2, :] * v + cw[2:3, :] * v_next + cb_ref[...]
    yc = bg_ref[...].astype(F32) * conv
    yc = yc * lax.rsqrt(jnp.mean(yc * yc, axis=-1, keepdims=True) + EPS) * gc_ref[...]
    y_ref[:, 0:dc] = (yc * _silu(zc_ref[...].astype(F32))).astype(y_ref.dtype)

    for r, od_ref, ld_ref, onat_ref, lnat_ref in zip(dilations, od_refs, ld_refs, onat_refs, lnat_refs):
        for res in range(r):
            rows = pl.ds(res, tr // r, stride=r)
            for p in range(n_pairs):
                onat_ref[p, rows, :] = od_ref[res, :, p * LANES:(p + 1) * LANES]
            lnat_ref[rows, :] = ld_ref[res]

    lses = [l1_ref[...]] + [ref[...] for ref in lnat_refs]
    lmax = functools.reduce(jnp.maximum, lses)
    exps = [jnp.exp(l - lmax) for l in lses]
    inv = 1.0 / functools.reduce(jnp.add, exps)
    alphas = [e * inv for e in exps]
    lane = lax.broadcasted_iota(jnp.int32, (1, LANES), 1)
    even_head = lane < HEAD_DIM
    ssq = jnp.zeros((tr, 1), F32)
    for p in range(n_pairs):
        cols = slice(p * LANES, (p + 1) * LANES)
        h0, h1 = 2 * p, 2 * p + 1
        branch_out = [o1_ref[:, cols]] + [ref[p] for ref in onat_refs]
        om = None
        for a, o in zip(alphas, branch_out):
            term = jnp.where(even_head, a[:, h0:h0 + 1], a[:, h1:h1 + 1]) * o
            om = term if om is None else om + term
        om_ref[:, cols] = om
        ssq = ssq + jnp.sum(om * om, axis=-1, keepdims=True)
    rstd = lax.rsqrt(ssq * (1.0 / da) + EPS)
    ya = om_ref[...] * rstd * ga_ref[...]
    y_ref[:, dc:dc + da] = (ya * _silu(za_ref[...].astype(F32))).astype(y_ref.dtype)


def _mix(proj, o1, l1, outs_d, lses_d, dilations, conv_w, conv_b, g_conv, g_attn, seq, d_conv, d_attn):
    m = proj.shape[0]
    tr = 256
    tps = seq // tr
    halo = BF16_SUBLANES
    hb = tr // halo
    n_halo_blocks = m // halo
    n_pairs = d_attn // LANES
    u_c, b_c, c_c, zc_c = 0, 1, 2, 3
    za_c = (4 * d_conv + 3 * d_attn) // d_attn

    def main(col):
        return pl.BlockSpec((tr, d_conv), lambda i: (i, col))

    def before(col):
        return pl.BlockSpec((halo, d_conv), lambda i: (jnp.maximum(i * hb - 1, 0), col))

    def after(col):
        return pl.BlockSpec((halo, d_conv), lambda i: (jnp.minimum((i + 1) * hb, n_halo_blocks - 1), col))

    def row_vec(n, rows=1):
        return pl.BlockSpec((rows, n), lambda i: (0, 0))

    def res_major(r, width):
        return pl.BlockSpec((None, r, tr // r, width), lambda i: (i // tps, 0, i % tps, 0))

    kern = functools.partial(_mix_kernel, tiles_per_seq=tps, dilations=dilations)
    nd = len(dilations)
    vmem = (2 * (5 * tr * d_conv * 2 + (1 + nd) * tr * (d_attn + LANES) * 4 + tr * (d_conv + d_attn) * 2)
            + (1 + nd) * tr * (d_attn + LANES) * 4 + (16 << 20))
    return pl.pallas_call(
        kern,
        grid=(m // tr,),
        in_specs=[main(u_c), main(b_c), main(c_c), main(zc_c),
                  pl.BlockSpec((tr, d_attn), lambda i: (i, za_c)),
                  before(u_c), before(c_c), after(u_c), after(c_c),
                  pl.BlockSpec((tr, d_attn), lambda i: (i, 0)),
                  pl.BlockSpec((tr, LANES), lambda i: (i, 0)),
                  row_vec(d_conv, CONV_WIDTH), row_vec(d_conv), row_vec(d_conv), row_vec(d_attn)]
        + [res_major(r, d_attn) for r in dilations] + [res_major(r, LANES) for r in dilations],
        out_specs=pl.BlockSpec((tr, d_conv + d_attn), lambda i: (i, 0)),
        out_shape=jax.ShapeDtypeStruct((m, d_conv + d_attn), BF16),
        scratch_shapes=[pltpu.VMEM((tr, d_attn), F32)]
        + [pltpu.VMEM((n_pairs, tr, LANES), F32) for _ in dilations]
        + [pltpu.VMEM((tr, LANES), F32) for _ in dilations],
        compiler_params=_params(("arbitrary",), vmem),
        name="mix_groups",
    )(proj, proj, proj, proj, proj, proj, proj, proj, proj, o1, l1,
      conv_w, conv_b.reshape(1, d_conv), g_conv.reshape(1, d_conv), g_attn.reshape(1, d_attn),
      *outs_d, *lses_d)


def _outproj_kernel(y_ref, w_ref, x_ref, gate_ref, g_ref, o_ref):
    y = jnp.dot(y_ref[...], w_ref[...], preferred_element_type=F32)
    yn = y * lax.rsqrt(jnp.mean(y * y, axis=-1, keepdims=True) + EPS) * g_ref[...]
    o_ref[...] = x_ref[...] + gate_ref[...] * yn


def _out_projection(y, w_out_bf16, x2d, mod3, g_post, seq):
    m, d = x2d.shape
    k = y.shape[1]
    tm = 512
    tiles_per_seq = seq // tm
    vmem = 2 * (tm * k * 2 + k * d * 2 + 2 * tm * d * 4) + tm * d * 4 + (8 << 20)
    return pl.pallas_call(
        _outproj_kernel,
        grid=(m // tm,),
        in_specs=[pl.BlockSpec((tm, k), lambda i: (i, 0)),
                  pl.BlockSpec((k, d), lambda i: (0, 0)),
                  pl.BlockSpec((tm, d), lambda i: (i, 0)),
                  pl.BlockSpec((None, 1, d), lambda i: (i // tiles_per_seq, 0, 2)),
                  pl.BlockSpec((1, d), lambda i: (0, 0))],
        out_specs=pl.BlockSpec((tm, d), lambda i: (i, 0)),
        out_shape=jax.ShapeDtypeStruct((m, d), F32),
        compiler_params=_params(("arbitrary",), vmem),
        name="out_proj",
    )(y, w_out_bf16, x2d, mod3, g_post.reshape(1, d))


def _layer(x, mod, g_pre, w_in, conv_w, conv_b, g_conv, g_attn, w_out, g_post):
    bsz, seq, d = x.shape
    d_conv = conv_w.shape[1]
    d_attn = g_attn.shape[0]
    n_proj = w_in.shape[1]
    n_heads = d_attn // HEAD_DIM
    q_col = 4 * d_conv
    assert d_attn % LANES == 0 and d_conv == d_attn and n_proj == 4 * d_conv + 4 * d_attn
    windows = {r: w for w, r in DILATED_BRANCHES}
    dilations = tuple(r for _, r in DILATED_BRANCHES if r > 1)
    assert set(windows) == {1, *dilations}

    x2d = x.reshape(bsz * seq, d)
    mod3 = mod.reshape(bsz, 1, 3 * d)
    proj, *qkv_d = _in_projection(x2d, mod3, g_pre, w_in.astype(BF16), bsz, seq, q_col, d_attn, dilations)

    slopes = 2.0 ** (-8.0 * jnp.arange(1, n_heads + 1, dtype=F32) / n_heads)
    o1, l1 = _attention_branch(proj.reshape(bsz, seq, n_proj), slopes, q_col // d_attn, d_attn, windows[1], 1)
    outs_d, lses_d = [], []
    for r, qkv in zip(dilations, qkv_d):
        o, lse = _attention_branch(qkv.reshape(bsz * r, seq // r, 3 * d_attn), slopes, 0, d_attn, windows[r], r)
        outs_d.append(o.reshape(bsz, r, seq // r, d_attn))
        lses_d.append(lse.reshape(bsz, r, seq // r, LANES))

    y = _mix(proj, o1.reshape(bsz * seq, d_attn), l1.reshape(bsz * seq, LANES), outs_d, lses_d, dilations,
             conv_w, conv_b, g_conv, g_attn, seq, d_conv, d_attn)
    out = _out_projection(y, w_out.astype(BF16), x2d, mod3, g_post, seq)
    return out.reshape(bsz, seq, d)


def kernel(x, c, w_ada, b_ada, g_pre, w_in, conv_w, conv_b, g_conv, g_attn, w_out, g_post):
    for layer in range(w_ada.shape[0]):
        mod = _modulation(c, w_ada[layer], b_ada[layer])
        x = _layer(x, mod, g_pre[layer], w_in[layer], conv_w[layer], conv_b[layer],
                   g_conv[layer], g_attn[layer], w_out[layer], g_post[layer])
    return x
```

```python
import functools

import jax
import jax.numpy as jnp
from jax import lax
from jax.experimental import pallas as pl
from jax.experimental.pallas import tpu as pltpu

HEAD_DIM = 64
CONV_WIDTH = 3
DILATED_BRANCHES = ((128, 1), (512, 4), (2048, 16))
EPS = 1e-6
NEG_INF = -1e30

LANES = 128
BF16_SUBLANES = 16
V7X_VMEM_LIMIT = 56 * 1024 * 1024

F32 = jnp.float32
BF16 = jnp.bfloat16


def _silu(v):
    return v / (1.0 + jnp.exp(-v))


def _params(semantics, vmem_bytes):
    return pltpu.CompilerParams(dimension_semantics=semantics,
                                vmem_limit_bytes=min(int(vmem_bytes), V7X_VMEM_LIMIT))


def _mod_kernel(c_ref, w_ref, b_ref, o_ref):
    c_act = _silu(c_ref[...]).astype(BF16)
    w = w_ref[...].astype(BF16)
    o_ref[...] = jnp.dot(c_act, w, preferred_element_type=F32) + b_ref[...]


def _modulation(c, w_ada, b_ada):
    bsz, d = c.shape
    n = w_ada.shape[1]
    tn = 512
    return pl.pallas_call(
        _mod_kernel,
        grid=(n // tn,),
        in_specs=[pl.BlockSpec((bsz, d), lambda j: (0, 0)),
                  pl.BlockSpec((d, tn), lambda j: (0, j)),
                  pl.BlockSpec((1, tn), lambda j: (0, j))],
        out_specs=pl.BlockSpec((bsz, tn), lambda j: (0, j)),
        out_shape=jax.ShapeDtypeStruct((bsz, n), F32),
        compiler_params=_params(("arbitrary",), 4 * d * tn * 4),
        name="adaln_mod",
    )(c, w_ada, b_ada.reshape(1, n))


def _inproj_kernel(x_ref, shift_ref, scale_ref, g_ref, w_ref, o_ref, *rest,
                   dilations, q_block, q_scale, chunk):
    res_refs, (h_ref, stage_refs) = rest[:len(dilations)], (rest[len(dilations)], rest[len(dilations) + 1:])
    tm, tn = o_ref.shape
    n_chunks = tm // chunk
    n_slabs = tn // LANES
    j = pl.program_id(1)
    is_qkv = (j >= q_block) & (j < q_block + 3)

    def rows_of(ck):
        return slice(ck * chunk, (ck + 1) * chunk)

    @pl.when(j == 0)
    def _():
        g = g_ref[...]
        one_plus_scale = 1.0 + scale_ref[...]
        shift = shift_ref[...]
        for ck in range(n_chunks):
            xc = x_ref[rows_of(ck), :]
            ms = jnp.mean(xc * xc, axis=-1, keepdims=True)
            y = xc * lax.rsqrt(ms + EPS) * g
            hc = (y * one_plus_scale + shift).astype(BF16)
            h_ref[rows_of(ck), :] = hc
            acc = jnp.dot(hc, w_ref[...], preferred_element_type=F32)
            o_ref[rows_of(ck), :] = acc.astype(o_ref.dtype)

    @pl.when((j != 0) & jnp.logical_not(is_qkv))
    def _():
        acc = jnp.dot(h_ref[...], w_ref[...], preferred_element_type=F32)
        o_ref[...] = acc.astype(o_ref.dtype)

    @pl.when(is_qkv)
    def _():
        fac = jnp.where(j == q_block, q_scale, 1.0).astype(F32)
        for ck in range(n_chunks):
            acc = jnp.dot(h_ref[rows_of(ck), :], w_ref[...], preferred_element_type=F32) * fac
            o_ref[rows_of(ck), :] = acc.astype(o_ref.dtype)
            slot = ck % 2
            for c in range(n_slabs):
                stage_refs[0][slot, c] = acc[:, c * LANES:(c + 1) * LANES]
            for c in range(n_slabs):
                lanes = slice(c * LANES, (c + 1) * LANES)
                classes = {0: (stage_refs[0].at[slot, c], chunk)}
                r_prev = 1
                for lvl, (r, res_ref) in enumerate(zip(dilations, res_refs)):
                    step = r // r_prev
                    new_classes = {}
                    for res_prev, (src, n_rows) in classes.items():
                        for k in range(step):
                            res = res_prev + r_prev * k
                            n_out = n_rows // step
                            rows = src[pl.ds(k, n_out, stride=step), :]
                            res_ref[res, ck * n_out:(ck + 1) * n_out, lanes] = rows.astype(res_ref.dtype)
                            if lvl + 1 < len(dilations):
                                dst = stage_refs[lvl + 1].at[slot, c, res]
                                dst[...] = rows
                                new_classes[res] = (dst, n_out)
                    classes, r_prev = new_classes, r


def _in_projection(x2d, mod3, g_pre, w_in_bf16, bsz, seq, q_col, d_attn, dilations):
    m, d = x2d.shape
    n = w_in_bf16.shape[1]
    tm, tn = 1024, d_attn
    tps = seq // tm
    q_block = q_col // tn
    chunk = 256
    assert q_block > 0 and all(b % a == 0 for a, b in zip((1,) + dilations, dilations))
    kern = functools.partial(_inproj_kernel, dilations=dilations, q_block=q_block,
                             q_scale=HEAD_DIM ** -0.5, chunk=chunk)
    stage_shapes = [pltpu.VMEM((2, tn // LANES, chunk, LANES), F32)]
    stage_shapes += [pltpu.VMEM((2, tn // LANES, r, chunk // r, LANES), F32) for r in dilations[:-1]]

    def res_spec(r):
        return pl.BlockSpec((None, r, tm // r, tn),
                            lambda i, j: (i // tps, 0, i % tps, jnp.clip(j - q_block, 0, 2)))

    vmem = (2 * tm * d * 4 + 2 * d * tn * 2 + 2 * (1 + len(dilations)) * tm * tn * 2
            + tm * d * 2 + len(dilations) * 2 * chunk * tn * 4 + tm * tn * 4 + (10 << 20))
    return pl.pallas_call(
        kern,
        grid=(m // tm, n // tn),
        in_specs=[pl.BlockSpec((tm, d), lambda i, j: (i, 0)),
                  pl.BlockSpec((None, 1, d), lambda i, j: (i // tps, 0, 0)),
                  pl.BlockSpec((None, 1, d), lambda i, j: (i // tps, 0, 1)),
                  pl.BlockSpec((1, d), lambda i, j: (0, 0)),
                  pl.BlockSpec((d, tn), lambda i, j: (0, j))],
        out_specs=[pl.BlockSpec((tm, tn), lambda i, j: (i, j))] + [res_spec(r) for r in dilations],
        out_shape=[jax.ShapeDtypeStruct((m, n), BF16)]
        + [jax.ShapeDtypeStruct((bsz, r, seq // r, 3 * d_attn), BF16) for r in dilations],
        scratch_shapes=[pltpu.VMEM((tm, d), BF16)] + stage_shapes,
        compiler_params=_params(("arbitrary", "arbitrary"), vmem),
        name="in_proj",
    )(x2d, mod3, mod3, g_pre.reshape(1, d), w_in_bf16)


def _attn_kernel(slopes_ref, q_ref, k_ref, v_ref, o_ref, st_ref, bias_ref, *,
                 seq_len, dilation, half_window, tq, tk, unroll):
    bb, lq, width = q_ref.shape
    n_pairs = width // LANES
    n_qt = lq // tq
    n_var = bias_ref.shape[0]
    max_ks = seq_len - tk
    first = (pl.program_id(0) == 0) & (pl.program_id(1) == 0)

    @pl.when(first)
    def _():
        qi = lax.broadcasted_iota(jnp.int32, (tq, tk), 0)
        ki = lax.broadcasted_iota(jnp.int32, (tq, tk), 1)
        for var in range(n_var):
            delta = (0, half_window, tk - tq)[var]
            off = jnp.abs(ki - qi - delta)
            valid = off <= half_window
            dist = (dilation * off).astype(F32)
            for h in range(2 * n_pairs):
                bias = jnp.where(valid, -slopes_ref[h] * dist, NEG_INF)
                r0 = (h % 2) * tq
                bias_ref[var, h // 2, r0:r0 + tq, :] = bias

    lane = lax.broadcasted_iota(jnp.int32, (1, LANES), 1)
    even_head = lane < HEAD_DIM
    keep_even = even_head.astype(BF16)
    keep_odd = 1.0 - keep_even
    q_tile0 = pl.program_id(1) * n_qt

    def tile(b, jt):
        qs_local = jt * tq if isinstance(jt, int) else pl.multiple_of(jt * tq, tq)
        qs = (q_tile0 + jt) * tq
        ks = pl.multiple_of(jnp.clip(qs - half_window, 0, max_ks), half_window)
        if n_var == 1:
            var = 0
        else:
            var = jnp.where(qs == 0, 0, jnp.where(qs == seq_len - tq, 2, 1))
        m_tile = jnp.zeros((tq, LANES), F32)
        d_tile = jnp.ones((tq, LANES), F32)
        for p in range(n_pairs):
            cols = slice(p * LANES, (p + 1) * LANES)
            q2 = q_ref[b, pl.ds(qs_local, tq), cols]
            kw = k_ref[b, pl.ds(ks, tk), cols]
            vw = v_ref[b, pl.ds(ks, tk), cols]
            qq = jnp.concatenate([q2 * keep_even, q2 * keep_odd], axis=0)
            s = lax.dot_general(qq, kw, (((1,), (1,)), ((), ())), preferred_element_type=F32)
            s = s + bias_ref[var, p]
            m = jnp.max(s, axis=-1, keepdims=True)
            e = jnp.exp(s - m)
            den = jnp.sum(e, axis=-1, keepdims=True)
            pv = jnp.dot(e.astype(BF16), vw, preferred_element_type=F32)
            o_ref[b, pl.ds(qs_local, tq), cols] = jnp.where(even_head, pv[:tq], pv[tq:])
            m_tile = jnp.where(lane == 2 * p, m[:tq], jnp.where(lane == 2 * p + 1, m[tq:], m_tile))
            d_tile = jnp.where(lane == 2 * p, den[:tq], jnp.where(lane == 2 * p + 1, den[tq:], d_tile))
        st_ref[b, pl.ds(qs_local, tq), 0:LANES] = m_tile
        st_ref[b, pl.ds(qs_local, tq), LANES:2 * LANES] = d_tile

    n_tiles = bb * n_qt
    group = min(unroll, n_tiles)

    def tile_group(it, carry):
        for u in range(group):
            idx = it * group + u
            tile(*((idx, 0) if n_qt == 1 else (idx // n_qt, idx % n_qt)))
        return carry

    if n_tiles == group:
        tile_group(0, 0)
    else:
        lax.fori_loop(0, n_tiles // group, tile_group, 0)


def _attention_branch(qkv, slopes, q_block, d_attn, window, dilation):
    n_seq, seq_len, _ = qkv.shape
    half_window = (window // 2) // dilation
    tq = 2 * half_window
    tk = min(2 * tq, seq_len)
    lq = min(seq_len, 512)
    bb = max(1, min(n_seq, 1024 // seq_len))
    n_var = 1 if tk == seq_len else 3
    n_pairs = d_attn // LANES

    kern = functools.partial(_attn_kernel, seq_len=seq_len, dilation=dilation, half_window=half_window,
                             tq=tq, tk=tk, unroll=4)
    vmem = (2 * bb * (lq + 2 * seq_len) * d_attn * 2 + 2 * bb * lq * (d_attn + 2 * LANES) * 4
            + n_var * n_pairs * 2 * tq * tk * 4 + (12 << 20))
    return pl.pallas_call(
        kern,
        grid=(n_seq // bb, seq_len // lq),
        in_specs=[pl.BlockSpec(memory_space=pltpu.SMEM),
                  pl.BlockSpec((bb, lq, d_attn), lambda b, t: (b, t, q_block)),
                  pl.BlockSpec((bb, seq_len, d_attn), lambda b, t: (b, 0, q_block + 1)),
                  pl.BlockSpec((bb, seq_len, d_attn), lambda b, t: (b, 0, q_block + 2))],
        out_specs=[pl.BlockSpec((bb, lq, d_attn), lambda b, t: (b, t, 0)),
                   pl.BlockSpec((bb, lq, 2 * LANES), lambda b, t: (b, t, 0))],
        out_shape=[jax.ShapeDtypeStruct((n_seq, seq_len, d_attn), F32),
                   jax.ShapeDtypeStruct((n_seq, seq_len, 2 * LANES), F32)],
        scratch_shapes=[pltpu.VMEM((n_var, n_pairs, 2 * tq, tk), F32)],
        compiler_params=_params(("arbitrary", "arbitrary"), vmem),
        name=f"dilated_attn_r{dilation}",
    )(slopes, qkv, qkv, qkv)


def _mix_kernel(u_ref, bg_ref, cg_ref, zc_ref, za_ref, up_ref, cp_ref, un_ref, cn_ref,
                o1_ref, s1_ref, cw_ref, cb_ref, gc_ref, ga_ref, *rest, tiles_per_seq, dilations):
    nd = len(dilations)
    od_refs, sd_refs = rest[:nd], rest[nd:2 * nd]
    y_ref, expand_ref = rest[2 * nd], rest[2 * nd + 1]
    onat_refs, snat_refs = rest[2 * nd + 2:3 * nd + 2], rest[3 * nd + 2:]
    tr, dc = u_ref.shape
    da = o1_ref.shape[1]
    n_pairs = da // LANES
    i = pl.program_id(0)
    t_in_seq = i % tiles_per_seq

    @pl.when(i == 0)
    def _():
        k = lax.broadcasted_iota(jnp.int32, expand_ref.shape, 0)
        c = lax.broadcasted_iota(jnp.int32, expand_ref.shape, 1)
        expand_ref[...] = jnp.where(k % LANES == c // HEAD_DIM, 1.0, 0.0).astype(expand_ref.dtype)

    v = cg_ref[...].astype(F32) * u_ref[...].astype(F32)
    last = BF16_SUBLANES - 1
    v_before = cp_ref[last:last + 1, :].astype(F32) * up_ref[last:last + 1, :].astype(F32)
    v_after = cn_ref[0:1, :].astype(F32) * un_ref[0:1, :].astype(F32)
    v_before = jnp.where(t_in_seq == 0, 0.0, v_before)
    v_after = jnp.where(t_in_seq == tiles_per_seq - 1, 0.0, v_after)
    row = lax.broadcasted_iota(jnp.int32, (tr, 1), 0)
    v_prev = jnp.where(row == 0, v_before, pltpu.roll(v, 1, axis=0))
    v_next = jnp.where(row == tr - 1, v_after, pltpu.roll(v, tr - 1, axis=0))
    cw = cw_ref[...]
    conv = cw[0:1, :] * v_prev + cw[1:2, :] * v + cw[2:3, :] * v_next + cb_ref[...]
    yc = bg_ref[...].astype(F32) * conv
    yc = yc * lax.rsqrt(jnp.mean(yc * yc, axis=-1, keepdims=True) + EPS) * gc_ref[...]
    y_ref[:, 0:dc] = (yc * _silu(zc_ref[...].astype(F32))).astype(y_ref.dtype)

    for r, od_ref, sd_ref, onat_ref, snat_ref in zip(dilations, od_refs, sd_refs, onat_refs, snat_refs):
        for res in range(r):
            rows = pl.ds(res, tr // r, stride=r)
            for p in range(n_pairs):
                onat_ref[p, rows, :] = od_ref[res, :, p * LANES:(p + 1) * LANES]
            for half in range(2):
                snat_ref[half, rows, :] = sd_ref[res, :, half * LANES:(half + 1) * LANES]

    ms = [s1_ref[:, 0:LANES]] + [ref[0] for ref in snat_refs]
    ds = [s1_ref[:, LANES:2 * LANES]] + [ref[1] for ref in snat_refs]
    mmax = functools.reduce(jnp.maximum, ms)
    es = [jnp.exp(m - mmax) for m in ms]
    inv = 1.0 / functools.reduce(jnp.add, [e * d for e, d in zip(es, ds)])
    branch_out = [o1_ref[...]] + [jnp.concatenate([ref[p] for p in range(n_pairs)], axis=1) for ref in onat_refs]
    om = None
    for e, o in zip(es, branch_out):
        w = e * inv
        w_hi = w.astype(BF16)
        w_lo = (w - w_hi.astype(F32)).astype(BF16)
        wb = jnp.dot(jnp.concatenate([w_hi, w_lo], axis=1), expand_ref[...], preferred_element_type=F32)
        om = wb * o if om is None else om + wb * o
    rstd = lax.rsqrt(jnp.mean(om * om, axis=-1, keepdims=True) + EPS)
    ya = om * rstd * ga_ref[...]
    y_ref[:, dc:dc + da] = (ya * _silu(za_ref[...].astype(F32))).astype(y_ref.dtype)


def _mix(proj, o1, s1, outs_d, stats_d, dilations, conv_w, conv_b, g_conv, g_attn, seq, d_conv, d_attn):
    m = proj.shape[0]
    tr = 256
    tps = seq // tr
    halo = BF16_SUBLANES
    hb = tr // halo
    n_halo_blocks = m // halo
    n_pairs = d_attn // LANES
    u_c, b_c, c_c, zc_c = 0, 1, 2, 3
    za_c = (4 * d_conv + 3 * d_attn) // d_attn

    def main(col):
        return pl.BlockSpec((tr, d_conv), lambda i: (i, col))

    def before(col):
        return pl.BlockSpec((halo, d_conv), lambda i: (jnp.maximum(i * hb - 1, 0), col))

    def after(col):
        return pl.BlockSpec((halo, d_conv), lambda i: (jnp.minimum((i + 1) * hb, n_halo_blocks - 1), col))

    def row_vec(n, rows=1):
        return pl.BlockSpec((rows, n), lambda i: (0, 0))

    def res_major(r, width):
        return pl.BlockSpec((None, r, tr // r, width), lambda i: (i // tps, 0, i % tps, 0))

    kern = functools.partial(_mix_kernel, tiles_per_seq=tps, dilations=dilations)
    nd = len(dilations)
    vmem = (2 * (5 * tr * d_conv * 2 + (1 + nd) * tr * (d_attn + 2 * LANES) * 4 + tr * (d_conv + d_attn) * 2)
            + nd * tr * (d_attn + 2 * LANES) * 4 + 2 * LANES * d_attn * 2 + (16 << 20))
    return pl.pallas_call(
        kern,
        grid=(m // tr,),
        in_specs=[main(u_c), main(b_c), main(c_c), main(zc_c),
                  pl.BlockSpec((tr, d_attn), lambda i: (i, za_c)),
                  before(u_c), before(c_c), after(u_c), after(c_c),
                  pl.BlockSpec((tr, d_attn), lambda i: (i, 0)),
                  pl.BlockSpec((tr, 2 * LANES), lambda i: (i, 0)),
                  row_vec(d_conv, CONV_WIDTH), row_vec(d_conv), row_vec(d_conv), row_vec(d_attn)]
        + [res_major(r, d_attn) for r in dilations] + [res_major(r, 2 * LANES) for r in dilations],
        out_specs=pl.BlockSpec((tr, d_conv + d_attn), lambda i: (i, 0)),
        out_shape=jax.ShapeDtypeStruct((m, d_conv + d_attn), BF16),
        scratch_shapes=[pltpu.VMEM((2 * LANES, d_attn), BF16)]
        + [pltpu.VMEM((n_pairs, tr, LANES), F32) for _ in dilations]
        + [pltpu.VMEM((2, tr, LANES), F32) for _ in dilations],
        compiler_params=_params(("arbitrary",), vmem),
        name="mix_groups",
    )(proj, proj, proj, proj, proj, proj, proj, proj, proj, o1, s1,
      conv_w, conv_b.reshape(1, d_conv), g_conv.reshape(1, d_conv), g_attn.reshape(1, d_attn),
      *outs_d, *stats_d)


def _outproj_kernel(y_ref, w_ref, x_ref, gate_ref, g_ref, o_ref):
    y = jnp.dot(y_ref[...], w_ref[...], preferred_element_type=F32)
    yn = y * lax.rsqrt(jnp.mean(y * y, axis=-1, keepdims=True) + EPS) * g_ref[...]
    o_ref[...] = x_ref[...] + gate_ref[...] * yn


def _out_projection(y, w_out_bf16, x2d, mod3, g_post, seq):
    m, d = x2d.shape
    k = y.shape[1]
    tm = 512
    tiles_per_seq = seq // tm
    vmem = 2 * (tm * k * 2 + k * d * 2 + 2 * tm * d * 4) + tm * d * 4 + (8 << 20)
    return pl.pallas_call(
        _outproj_kernel,
        grid=(m // tm,),
        in_specs=[pl.BlockSpec((tm, k), lambda i: (i, 0)),
                  pl.BlockSpec((k, d), lambda i: (0, 0)),
                  pl.BlockSpec((tm, d), lambda i: (i, 0)),
                  pl.BlockSpec((None, 1, d), lambda i: (i // tiles_per_seq, 0, 2)),
                  pl.BlockSpec((1, d), lambda i: (0, 0))],
        out_specs=pl.BlockSpec((tm, d), lambda i: (i, 0)),
        out_shape=jax.ShapeDtypeStruct((m, d), F32),
        compiler_params=_params(("arbitrary",), vmem),
        name="out_proj",
    )(y, w_out_bf16, x2d, mod3, g_post.reshape(1, d))


def _layer(x, mod, g_pre, w_in, conv_w, conv_b, g_conv, g_attn, w_out, g_post):
    bsz, seq, d = x.shape
    d_conv = conv_w.shape[1]
    d_attn = g_attn.shape[0]
    n_proj = w_in.shape[1]
    n_heads = d_attn // HEAD_DIM
    q_col = 4 * d_conv
    assert d_attn % LANES == 0 and d_conv == d_attn and n_proj == 4 * d_conv + 4 * d_attn
    windows = {r: w for w, r in DILATED_BRANCHES}
    dilations = tuple(r for _, r in DILATED_BRANCHES if r > 1)
    assert set(windows) == {1, *dilations}

    x2d = x.reshape(bsz * seq, d)
    mod3 = mod.reshape(bsz, 1, 3 * d)
    proj, *qkv_d = _in_projection(x2d, mod3, g_pre, w_in.astype(BF16), bsz, seq, q_col, d_attn, dilations)

    slopes = 2.0 ** (-8.0 * jnp.arange(1, n_heads + 1, dtype=F32) / n_heads)
    o1, s1 = _attention_branch(proj.reshape(bsz, seq, n_proj), slopes, q_col // d_attn, d_attn, windows[1], 1)
    outs_d, stats_d = [], []
    for r, qkv in zip(dilations, qkv_d):
        o, st = _attention_branch(qkv.reshape(bsz * r, seq // r, 3 * d_attn), slopes, 0, d_attn, windows[r], r)
        outs_d.append(o.reshape(bsz, r, seq // r, d_attn))
        stats_d.append(st.reshape(bsz, r, seq // r, 2 * LANES))

    y = _mix(proj, o1.reshape(bsz * seq, d_attn), s1.reshape(bsz * seq, 2 * LANES), outs_d, stats_d, dilations,
             conv_w, conv_b, g_conv, g_attn, seq, d_conv, d_attn)
    out = _out_projection(y, w_out.astype(BF16), x2d, mod3, g_post, seq)
    return out.reshape(bsz, seq, d)


def kernel(x, c, w_ada, b_ada, g_pre, w_in, conv_w, conv_b, g_conv, g_attn, w_out, g_post):
    for layer in range(w_ada.shape[0]):
        mod = _modulation(c, w_ada[layer], b_ada[layer])
        x = _layer(x, mod, g_pre[layer], w_in[layer], conv_w[layer], conv_b[layer],
                   g_conv[layer], g_attn[layer], w_out[layer], g_post[layer])
    return x
```

```python
import functools

import jax
import jax.numpy as jnp
from jax import lax
from jax.experimental import pallas as pl
from jax.experimental.pallas import tpu as pltpu

HEAD_DIM = 64
CONV_WIDTH = 3
DILATED_BRANCHES = ((128, 1), (512, 4), (2048, 16))
EPS = 1e-6
NEG_INF = -1e30

LANES = 128
BF16_SUBLANES = 16
V7X_VMEM_LIMIT = 56 * 1024 * 1024

F32 = jnp.float32
BF16 = jnp.bfloat16


def _silu(v):
    return v / (1.0 + jnp.exp(-v))


def _params(semantics, vmem_bytes):
    return pltpu.CompilerParams(dimension_semantics=semantics,
                                vmem_limit_bytes=min(int(vmem_bytes), V7X_VMEM_LIMIT))


def _mod_kernel(c_ref, w_ref, b_ref, o_ref):
    c_act = _silu(c_ref[...]).astype(BF16)
    w = w_ref[...].astype(BF16)
    o_ref[...] = jnp.dot(c_act, w, preferred_element_type=F32) + b_ref[...]


def _modulation(c, w_ada, b_ada):
    bsz, d = c.shape
    n = w_ada.shape[1]
    tn = 512
    return pl.pallas_call(
        _mod_kernel,
        grid=(n // tn,),
        in_specs=[pl.BlockSpec((bsz, d), lambda j: (0, 0)),
                  pl.BlockSpec((d, tn), lambda j: (0, j)),
                  pl.BlockSpec((1, tn), lambda j: (0, j))],
        out_specs=pl.BlockSpec((bsz, tn), lambda j: (0, j)),
        out_shape=jax.ShapeDtypeStruct((bsz, n), F32),
        compiler_params=_params(("arbitrary",), 4 * d * tn * 4),
        name="adaln_mod",
    )(c, w_ada, b_ada.reshape(1, n))


def _inproj_kernel(x_ref, shift_ref, scale_ref, g_ref, w_ref, o_ref, *rest,
                   dilations, q_block, q_scale, chunk):
    res_refs, (h_ref, stage_refs) = rest[:len(dilations)], (rest[len(dilations)], rest[len(dilations) + 1:])
    tm, tn = o_ref.shape
    n_chunks = tm // chunk
    n_slabs = tn // LANES
    j = pl.program_id(1)
    is_qkv = (j >= q_block) & (j < q_block + 3)

    def rows_of(ck):
        return slice(ck * chunk, (ck + 1) * chunk)

    @pl.when(j == 0)
    def _():
        g = g_ref[...]
        one_plus_scale = 1.0 + scale_ref[...]
        shift = shift_ref[...]
        for ck in range(n_chunks):
            xc = x_ref[rows_of(ck), :]
            ms = jnp.mean(xc * xc, axis=-1, keepdims=True)
            y = xc * lax.rsqrt(ms + EPS) * g
            hc = (y * one_plus_scale + shift).astype(BF16)
            h_ref[rows_of(ck), :] = hc
            acc = jnp.dot(hc, w_ref[...], preferred_element_type=F32)
            o_ref[rows_of(ck), :] = acc.astype(o_ref.dtype)

    @pl.when((j != 0) & jnp.logical_not(is_qkv))
    def _():
        acc = jnp.dot(h_ref[...], w_ref[...], preferred_element_type=F32)
        o_ref[...] = acc.astype(o_ref.dtype)

    @pl.when(is_qkv)
    def _():
        fac = jnp.where(j == q_block, q_scale, 1.0).astype(F32)
        for ck in range(n_chunks):
            acc = jnp.dot(h_ref[rows_of(ck), :], w_ref[...], preferred_element_type=F32) * fac
            o_ref[rows_of(ck), :] = acc.astype(o_ref.dtype)
            slot = ck % 2
            for c in range(n_slabs):
                stage_refs[0][slot, c] = acc[:, c * LANES:(c + 1) * LANES]
            for c in range(n_slabs):
                lanes = slice(c * LANES, (c + 1) * LANES)
                classes = {0: (stage_refs[0].at[slot, c], chunk)}
                r_prev = 1
                for lvl, (r, res_ref) in enumerate(zip(dilations, res_refs)):
                    step = r // r_prev
                    new_classes = {}
                    for res_prev, (src, n_rows) in classes.items():
                        for k in range(step):
                            res = res_prev + r_prev * k
                            n_out = n_rows // step
                            rows = src[pl.ds(k, n_out, stride=step), :]
                            res_ref[res, ck * n_out:(ck + 1) * n_out, lanes] = rows.astype(res_ref.dtype)
                            if lvl + 1 < len(dilations):
                                dst = stage_refs[lvl + 1].at[slot, c, res]
                                dst[...] = rows
                                new_classes[res] = (dst, n_out)
                    classes, r_prev = new_classes, r


def _in_projection(x2d, mod3, g_pre, w_in_bf16, bsz, seq, q_col, d_attn, dilations):
    m, d = x2d.shape
    n = w_in_bf16.shape[1]
    tm, tn = 1024, d_attn
    tps = seq // tm
    q_block = q_col // tn
    chunk = 256
    assert q_block > 0 and all(b % a == 0 for a, b in zip((1,) + dilations, dilations))
    kern = functools.partial(_inproj_kernel, dilations=dilations, q_block=q_block,
                             q_scale=HEAD_DIM ** -0.5, chunk=chunk)
    stage_shapes = [pltpu.VMEM((2, tn // LANES, chunk, LANES), F32)]
    stage_shapes += [pltpu.VMEM((2, tn // LANES, r, chunk // r, LANES), F32) for r in dilations[:-1]]

    def res_spec(r):
        return pl.BlockSpec((None, r, tm // r, tn),
                            lambda i, j: (i // tps, 0, i % tps, jnp.clip(j - q_block, 0, 2)))

    vmem = (2 * tm * d * 4 + 2 * d * tn * 2 + 2 * (1 + len(dilations)) * tm * tn * 2
            + tm * d * 2 + len(dilations) * 2 * chunk * tn * 4 + tm * tn * 4 + (10 << 20))
    return pl.pallas_call(
        kern,
        grid=(m // tm, n // tn),
        in_specs=[pl.BlockSpec((tm, d), lambda i, j: (i, 0)),
                  pl.BlockSpec((None, 1, d), lambda i, j: (i // tps, 0, 0)),
                  pl.BlockSpec((None, 1, d), lambda i, j: (i // tps, 0, 1)),
                  pl.BlockSpec((1, d), lambda i, j: (0, 0)),
                  pl.BlockSpec((d, tn), lambda i, j: (0, j))],
        out_specs=[pl.BlockSpec((tm, tn), lambda i, j: (i, j))] + [res_spec(r) for r in dilations],
        out_shape=[jax.ShapeDtypeStruct((m, n), BF16)]
        + [jax.ShapeDtypeStruct((bsz, r, seq // r, 3 * d_attn), BF16) for r in dilations],
        scratch_shapes=[pltpu.VMEM((tm, d), BF16)] + stage_shapes,
        compiler_params=_params(("arbitrary", "arbitrary"), vmem),
        name="in_proj",
    )(x2d, mod3, mod3, g_pre.reshape(1, d), w_in_bf16)


def _attn_kernel(slopes_ref, q_ref, k_ref, v_ref, o_ref, st_ref, bias_ref, *,
                 seq_len, dilation, half_window, tq, tk, bb, unroll):
    _, lq, width = q_ref.shape
    n_pairs = width // LANES
    n_qt = lq // tq
    n_var = bias_ref.shape[0]
    max_ks = seq_len - tk
    first = (pl.program_id(0) == 0) & (pl.program_id(1) == 0)

    @pl.when(first)
    def _():
        qi = lax.broadcasted_iota(jnp.int32, (tq, tk), 0)
        ki = lax.broadcasted_iota(jnp.int32, (tq, tk), 1)
        for var in range(n_var):
            delta = (0, half_window, tk - tq)[var]
            off = jnp.abs(ki - qi - delta)
            valid = off <= half_window
            dist = (dilation * off).astype(F32)
            for h in range(2 * n_pairs):
                bias = jnp.where(valid, -slopes_ref[h] * dist, NEG_INF)
                r0 = (h % 2) * tq
                bias_ref[var, h // 2, r0:r0 + tq, :] = bias

    lane = lax.broadcasted_iota(jnp.int32, (1, LANES), 1)
    even_head = lane < HEAD_DIM
    keep_even = even_head.astype(BF16)
    keep_odd = 1.0 - keep_even
    q_tile0 = pl.program_id(1) * n_qt

    def tile(b, jt):
        qs_local = jt * tq if isinstance(jt, int) else pl.multiple_of(jt * tq, tq)
        qs = (q_tile0 + jt) * tq
        ks = pl.multiple_of(jnp.clip(qs - half_window, 0, max_ks), half_window)
        if n_var == 1:
            var = 0
        else:
            var = jnp.where(qs == 0, 0, jnp.where(qs == seq_len - tq, 2, 1))
        if dilation == 1:
            out_rows = pl.ds(qs_local, tq)
        else:
            res = (pl.program_id(0) * bb + b) % dilation
            out_rows = pl.ds(res + dilation * qs, tq, stride=dilation)
        m_tile = jnp.zeros((tq, LANES), F32)
        d_tile = jnp.ones((tq, LANES), F32)
        for p in range(n_pairs):
            cols = slice(p * LANES, (p + 1) * LANES)
            q2 = q_ref[b, pl.ds(qs_local, tq), cols]
            kw = k_ref[b, pl.ds(ks, tk), cols]
            vw = v_ref[b, pl.ds(ks, tk), cols]
            qq = jnp.concatenate([q2 * keep_even, q2 * keep_odd], axis=0)
            s = lax.dot_general(qq, kw, (((1,), (1,)), ((), ())), preferred_element_type=F32)
            s = s + bias_ref[var, p]
            m = jnp.max(s, axis=-1, keepdims=True)
            e = jnp.exp(s - m)
            den = jnp.sum(e, axis=-1, keepdims=True)
            pv = jnp.dot(e.astype(BF16), vw, preferred_element_type=F32)
            o_ref[p, out_rows, :] = jnp.where(even_head, pv[:tq], pv[tq:])
            m_tile = jnp.where(lane == 2 * p, m[:tq], jnp.where(lane == 2 * p + 1, m[tq:], m_tile))
            d_tile = jnp.where(lane == 2 * p, den[:tq], jnp.where(lane == 2 * p + 1, den[tq:], d_tile))
        st_ref[0, out_rows, :] = m_tile
        st_ref[1, out_rows, :] = d_tile

    n_tiles = bb * n_qt
    group = min(unroll, n_tiles)

    def tile_group(it, carry):
        for u in range(group):
            idx = it * group + u
            tile(*((idx, 0) if n_qt == 1 else (idx // n_qt, idx % n_qt)))
        return carry

    if n_tiles == group:
        tile_group(0, 0)
    else:
        lax.fori_loop(0, n_tiles // group, tile_group, 0)


def _attention_branch(qkv, slopes, q_block, d_attn, window, dilation):
    n_seq, seq_len, _ = qkv.shape
    half_window = (window // 2) // dilation
    tq = 2 * half_window
    tk = min(2 * tq, seq_len)
    lq = min(seq_len, 512)
    bb = max(1, min(n_seq, 1024 // seq_len))
    n_var = 1 if tk == seq_len else 3
    n_pairs = d_attn // LANES
    bsz = n_seq // dilation
    seq = seq_len * dilation
    if dilation == 1:
        assert bb == 1
        out_rows = lq

        def out_index(b, t):
            return (b, 0, t, 0)
    else:
        assert lq == seq_len and dilation % bb == 0
        out_rows = seq

        def out_index(b, t):
            return ((b * bb) // dilation, 0, 0, 0)

    kern = functools.partial(_attn_kernel, seq_len=seq_len, dilation=dilation, half_window=half_window,
                             tq=tq, tk=tk, bb=bb, unroll=4)
    vmem = (2 * bb * (lq + 2 * seq_len) * d_attn * 2 + 2 * out_rows * (d_attn + 2 * LANES) * 4
            + n_var * n_pairs * 2 * tq * tk * 4 + (12 << 20))
    return pl.pallas_call(
        kern,
        grid=(n_seq // bb, seq_len // lq),
        in_specs=[pl.BlockSpec(memory_space=pltpu.SMEM),
                  pl.BlockSpec((bb, lq, d_attn), lambda b, t: (b, t, q_block)),
                  pl.BlockSpec((bb, seq_len, d_attn), lambda b, t: (b, 0, q_block + 1)),
                  pl.BlockSpec((bb, seq_len, d_attn), lambda b, t: (b, 0, q_block + 2))],
        out_specs=[pl.BlockSpec((None, n_pairs, out_rows, LANES), out_index),
                   pl.BlockSpec((None, 2, out_rows, LANES), out_index)],
        out_shape=[jax.ShapeDtypeStruct((bsz, n_pairs, seq, LANES), F32),
                   jax.ShapeDtypeStruct((bsz, 2, seq, LANES), F32)],
        scratch_shapes=[pltpu.VMEM((n_var, n_pairs, 2 * tq, tk), F32)],
        compiler_params=_params(("arbitrary", "arbitrary"), vmem),
        name=f"dilated_attn_r{dilation}",
    )(slopes, qkv, qkv, qkv)


def _mix_out_kernel(u_ref, bg_ref, cg_ref, zc_ref, za_ref, up_ref, cp_ref, un_ref, cn_ref,
                    cw_ref, cb_ref, gc_ref, ga_ref, w_ref, x_ref, gate_ref, gp_ref,
                    *rest, tiles_per_seq, n_chunks, n_branches):
    o_refs, st_refs = rest[:n_branches], rest[n_branches:2 * n_branches]
    out_ref, expand_ref = rest[2 * n_branches:]
    tr, dc = u_ref.shape
    n_pairs = o_refs[0].shape[0]
    rc = tr // n_chunks
    i = pl.program_id(0)
    t_in_seq = i % tiles_per_seq

    @pl.when(i == 0)
    def _():
        k = lax.broadcasted_iota(jnp.int32, expand_ref.shape, 0)
        c = lax.broadcasted_iota(jnp.int32, expand_ref.shape, 1)
        expand_ref[...] = jnp.where(k % LANES == c // HEAD_DIM, 1.0, 0.0).astype(expand_ref.dtype)

    ms = [ref[0] for ref in st_refs]
    ds = [ref[1] for ref in st_refs]
    mmax = functools.reduce(jnp.maximum, ms)
    es = [jnp.exp(m - mmax) for m in ms]
    inv = 1.0 / functools.reduce(jnp.add, [e * d for e, d in zip(es, ds)])
    weights = []
    for e in es:
        w = e * inv
        w_hi = w.astype(BF16)
        w_lo = (w - w_hi.astype(F32)).astype(BF16)
        weights.append(jnp.dot(jnp.concatenate([w_hi, w_lo], axis=1), expand_ref[...],
                               preferred_element_type=F32))

    last = BF16_SUBLANES - 1
    v_before = cp_ref[last:last + 1, :].astype(F32) * up_ref[last:last + 1, :].astype(F32)
    v_after = cn_ref[0:1, :].astype(F32) * un_ref[0:1, :].astype(F32)
    v_before = jnp.where(t_in_seq == 0, 0.0, v_before)
    v_after = jnp.where(t_in_seq == tiles_per_seq - 1, 0.0, v_after)
    vs = [cg_ref[k * rc:(k + 1) * rc, :].astype(F32) * u_ref[k * rc:(k + 1) * rc, :].astype(F32)
          for k in range(n_chunks)]
    row = lax.broadcasted_iota(jnp.int32, (rc, 1), 0)
    cw = cw_ref[...]

    for k in range(n_chunks):
        rows = slice(k * rc, (k + 1) * rc)
        above = v_before if k == 0 else vs[k - 1][rc - 1:rc, :]
        below = v_after if k == n_chunks - 1 else vs[k + 1][0:1, :]
        v_prev = jnp.where(row == 0, above, pltpu.roll(vs[k], 1, axis=0))
        v_next = jnp.where(row == rc - 1, below, pltpu.roll(vs[k], rc - 1, axis=0))
        conv = cw[0:1, :] * v_prev + cw[1:2, :] * vs[k] + cw[2:3, :] * v_next + cb_ref[...]
        yc = bg_ref[rows, :].astype(F32) * conv
        yc = yc * lax.rsqrt(jnp.mean(yc * yc, axis=-1, keepdims=True) + EPS) * gc_ref[...]
        y_conv = (yc * _silu(zc_ref[rows, :].astype(F32))).astype(BF16)

        branch_out = [jnp.concatenate([ref[p, rows, :] for p in range(n_pairs)], axis=1) for ref in o_refs]
        om = functools.reduce(jnp.add, [wb[rows, :] * o for wb, o in zip(weights, branch_out)])
        ya = om * lax.rsqrt(jnp.mean(om * om, axis=-1, keepdims=True) + EPS) * ga_ref[...]
        y_attn = (ya * _silu(za_ref[rows, :].astype(F32))).astype(BF16)

        y = jnp.dot(jnp.concatenate([y_conv, y_attn], axis=1), w_ref[...], preferred_element_type=F32)
        yn = y * lax.rsqrt(jnp.mean(y * y, axis=-1, keepdims=True) + EPS) * gp_ref[...]
        out_ref[rows, :] = x_ref[rows, :] + gate_ref[...] * yn


def _mix_out(proj, outs, stats, conv_w, conv_b, g_conv, g_attn, w_out_bf16, x2d, mod3, g_post,
             seq, d_conv, d_attn):
    m, d = x2d.shape
    tr = 256
    tps = seq // tr
    halo = BF16_SUBLANES
    hb = tr // halo
    n_halo_blocks = m // halo
    n_pairs = d_attn // LANES
    u_c, b_c, c_c, zc_c = 0, 1, 2, 3
    za_c = (4 * d_conv + 3 * d_attn) // d_attn

    def main(col, width=d_conv):
        return pl.BlockSpec((tr, width), lambda i: (i, col))

    def before(col):
        return pl.BlockSpec((halo, d_conv), lambda i: (jnp.maximum(i * hb - 1, 0), col))

    def after(col):
        return pl.BlockSpec((halo, d_conv), lambda i: (jnp.minimum((i + 1) * hb, n_halo_blocks - 1), col))

    def row_vec(n, rows=1):
        return pl.BlockSpec((rows, n), lambda i: (0, 0))

    def slabs(n):
        return pl.BlockSpec((None, n, tr, LANES), lambda i: (i // tps, 0, i % tps, 0))

    nb = len(outs)
    kern = functools.partial(_mix_out_kernel, tiles_per_seq=tps, n_chunks=2, n_branches=nb)
    vmem = (2 * (5 * tr * d_conv * 2 + nb * tr * (d_attn + 2 * LANES) * 4 + 2 * tr * d * 4)
            + 2 * (d_conv + d_attn) * d * 2 + 2 * LANES * d_attn * 2 + (16 << 20))
    return pl.pallas_call(
        kern,
        grid=(m // tr,),
        in_specs=[main(u_c), main(b_c), main(c_c), main(zc_c), main(za_c, d_attn),
                  before(u_c), before(c_c), after(u_c), after(c_c),
                  row_vec(d_conv, CONV_WIDTH), row_vec(d_conv), row_vec(d_conv), row_vec(d_attn),
                  pl.BlockSpec((d_conv + d_attn, d), lambda i: (0, 0)),
                  pl.BlockSpec((tr, d), lambda i: (i, 0)),
                  pl.BlockSpec((None, 1, d), lambda i: (i // tps, 0, 2)),
                  row_vec(d)]
        + [slabs(n_pairs)] * nb + [slabs(2)] * nb,
        out_specs=pl.BlockSpec((tr, d), lambda i: (i, 0)),
        out_shape=jax.ShapeDtypeStruct((m, d), F32),
        scratch_shapes=[pltpu.VMEM((2 * LANES, d_attn), BF16)],
        compiler_params=_params(("arbitrary",), vmem),
        name="mix_out_proj",
    )(proj, proj, proj, proj, proj, proj, proj, proj, proj,
      conv_w, conv_b.reshape(1, d_conv), g_conv.reshape(1, d_conv), g_attn.reshape(1, d_attn),
      w_out_bf16, x2d, mod3, g_post.reshape(1, d), *outs, *stats)


def _layer(x, mod, g_pre, w_in, conv_w, conv_b, g_conv, g_attn, w_out, g_post):
    bsz, seq, d = x.shape
    d_conv = conv_w.shape[1]
    d_attn = g_attn.shape[0]
    n_proj = w_in.shape[1]
    n_heads = d_attn // HEAD_DIM
    q_col = 4 * d_conv
    assert d_attn % LANES == 0 and d_conv == d_attn and n_proj == 4 * d_conv + 4 * d_attn
    windows = {r: w for w, r in DILATED_BRANCHES}
    dilations = tuple(r for _, r in DILATED_BRANCHES if r > 1)
    assert set(windows) == {1, *dilations}

    x2d = x.reshape(bsz * seq, d)
    mod3 = mod.reshape(bsz, 1, 3 * d)
    proj, *qkv_d = _in_projection(x2d, mod3, g_pre, w_in.astype(BF16), bsz, seq, q_col, d_attn, dilations)

    slopes = 2.0 ** (-8.0 * jnp.arange(1, n_heads + 1, dtype=F32) / n_heads)
    branches = [_attention_branch(proj.reshape(bsz, seq, n_proj), slopes, q_col // d_attn, d_attn, windows[1], 1)]
    for r, qkv in zip(dilations, qkv_d):
        branches.append(_attention_branch(qkv.reshape(bsz * r, seq // r, 3 * d_attn), slopes, 0, d_attn,
                                          windows[r], r))
    outs, stats = zip(*branches)

    out = _mix_out(proj, outs, stats, conv_w, conv_b, g_conv, g_attn, w_out.astype(BF16), x2d, mod3, g_post,
                   seq, d_conv, d_attn)
    return out.reshape(bsz, seq, d)


def kernel(x, c, w_ada, b_ada, g_pre, w_in, conv_w, conv_b, g_conv, g_attn, w_out, g_post):
    for layer in range(w_ada.shape[0]):
        mod = _modulation(c, w_ada[layer], b_ada[layer])
        x = _layer(x, mod, g_pre[layer], w_in[layer], conv_w[layer], conv_b[layer],
                   g_conv[layer], g_attn[layer], w_out[layer], g_post[layer])
    return x
```

```python
import functools

import jax
import jax.numpy as jnp
from jax import lax
from jax.experimental import pallas as pl
from jax.experimental.pallas import tpu as pltpu

HEAD_DIM = 64
CONV_WIDTH = 3
DILATED_BRANCHES = ((128, 1), (512, 4), (2048, 16))
EPS = 1e-6
NEG_INF = -1e30
LOG2_E = 1.4426950408889634

LANES = 128
BF16_SUBLANES = 16
V7X_VMEM_LIMIT = 56 * 1024 * 1024

F32 = jnp.float32
BF16 = jnp.bfloat16


def _silu(v):
    half = 0.5 * v
    return half + half * jnp.tanh(half)


def _params(semantics, vmem_bytes):
    return pltpu.CompilerParams(dimension_semantics=semantics,
                                vmem_limit_bytes=min(int(vmem_bytes), V7X_VMEM_LIMIT))


def _mod_kernel(c_ref, w_ref, b_ref, o_ref):
    c_act = _silu(c_ref[...]).astype(BF16)
    w = w_ref[...].astype(BF16)
    o_ref[...] = jnp.dot(c_act, w, preferred_element_type=F32) + b_ref[...]


def _modulation(c, w_ada, b_ada):
    bsz, d = c.shape
    n = w_ada.shape[1]
    tn = 512
    return pl.pallas_call(
        _mod_kernel,
        grid=(n // tn,),
        in_specs=[pl.BlockSpec((bsz, d), lambda j: (0, 0)),
                  pl.BlockSpec((d, tn), lambda j: (0, j)),
                  pl.BlockSpec((1, tn), lambda j: (0, j))],
        out_specs=pl.BlockSpec((bsz, tn), lambda j: (0, j)),
        out_shape=jax.ShapeDtypeStruct((bsz, n), F32),
        compiler_params=_params(("arbitrary",), 4 * d * tn * 4),
        name="adaln_mod",
    )(c, w_ada, b_ada.reshape(1, n))


def _inproj_kernel(x_ref, shift_ref, scale_ref, g_ref, w_ref, o_ref, *rest,
                   dilations, q_block, q_scale, chunk):
    res_refs, (h_ref, stage_refs) = rest[:len(dilations)], (rest[len(dilations)], rest[len(dilations) + 1:])
    tm, tn = o_ref.shape
    n_chunks = tm // chunk
    n_slabs = tn // LANES
    j = pl.program_id(1)
    is_qkv = (j >= q_block) & (j < q_block + 3)

    def rows_of(ck):
        return slice(ck * chunk, (ck + 1) * chunk)

    @pl.when(j == 0)
    def _():
        g = g_ref[...]
        one_plus_scale = 1.0 + scale_ref[...]
        shift = shift_ref[...]
        for ck in range(n_chunks):
            xc = x_ref[rows_of(ck), :]
            ms = jnp.mean(xc * xc, axis=-1, keepdims=True)
            y = xc * lax.rsqrt(ms + EPS) * g
            h_ref[rows_of(ck), :] = (y * one_plus_scale + shift).astype(BF16)
            acc = jnp.dot(h_ref[rows_of(ck), :], w_ref[...], preferred_element_type=F32)
            o_ref[rows_of(ck), :] = acc.astype(o_ref.dtype)

    @pl.when((j != 0) & jnp.logical_not(is_qkv))
    def _():
        acc = jnp.dot(h_ref[...], w_ref[...], preferred_element_type=F32)
        o_ref[...] = acc.astype(o_ref.dtype)

    @pl.when(is_qkv)
    def _():
        fac = jnp.where(j == q_block, q_scale, 1.0).astype(F32)
        for ck in range(n_chunks):
            acc = jnp.dot(h_ref[rows_of(ck), :], w_ref[...], preferred_element_type=F32) * fac
            o_ref[rows_of(ck), :] = acc.astype(o_ref.dtype)
            slot = ck % 2
            for c in range(n_slabs):
                stage_refs[0][slot, c] = acc[:, c * LANES:(c + 1) * LANES]
            for c in range(n_slabs):
                lanes = slice(c * LANES, (c + 1) * LANES)
                classes = {0: (stage_refs[0].at[slot, c], chunk)}
                r_prev = 1
                for lvl, (r, res_ref) in enumerate(zip(dilations, res_refs)):
                    step = r // r_prev
                    new_classes = {}
                    for res_prev, (src, n_rows) in classes.items():
                        for k in range(step):
                            res = res_prev + r_prev * k
                            n_out = n_rows // step
                            rows = src[pl.ds(k, n_out, stride=step), :]
                            res_ref[res, ck * n_out:(ck + 1) * n_out, lanes] = rows.astype(res_ref.dtype)
                            if lvl + 1 < len(dilations):
                                dst = stage_refs[lvl + 1].at[slot, c, res]
                                dst[...] = rows
                                new_classes[res] = (dst, n_out)
                    classes, r_prev = new_classes, r


def _in_projection(x2d, mod3, g_pre, w_in_bf16, bsz, seq, q_col, d_attn, dilations):
    m, d = x2d.shape
    n = w_in_bf16.shape[1]
    tm, tn = 1024, d_attn
    tps = seq // tm
    q_block = q_col // tn
    chunk = 256
    assert q_block > 0 and all(b % a == 0 for a, b in zip((1,) + dilations, dilations))
    kern = functools.partial(_inproj_kernel, dilations=dilations, q_block=q_block,
                             q_scale=HEAD_DIM ** -0.5 * LOG2_E, chunk=chunk)
    stage_shapes = [pltpu.VMEM((2, tn // LANES, chunk, LANES), F32)]
    stage_shapes += [pltpu.VMEM((2, tn // LANES, r, chunk // r, LANES), F32) for r in dilations[:-1]]

    def res_spec(r):
        return pl.BlockSpec((None, r, tm // r, tn),
                            lambda i, j: (i // tps, 0, i % tps, jnp.clip(j - q_block, 0, 2)))

    vmem = (2 * tm * d * 4 + 2 * d * tn * 2 + 2 * (1 + len(dilations)) * tm * tn * 2
            + tm * d * 2 + len(dilations) * 2 * chunk * tn * 4 + tm * tn * 4 + (10 << 20))
    return pl.pallas_call(
        kern,
        grid=(m // tm, n // tn),
        in_specs=[pl.BlockSpec((tm, d), lambda i, j: (i, 0)),
                  pl.BlockSpec((None, 1, d), lambda i, j: (i // tps, 0, 0)),
                  pl.BlockSpec((None, 1, d), lambda i, j: (i // tps, 0, 1)),
                  pl.BlockSpec((1, d), lambda i, j: (0, 0)),
                  pl.BlockSpec((d, tn), lambda i, j: (0, j))],
        out_specs=[pl.BlockSpec((tm, tn), lambda i, j: (i, j))] + [res_spec(r) for r in dilations],
        out_shape=[jax.ShapeDtypeStruct((m, n), BF16)]
        + [jax.ShapeDtypeStruct((bsz, r, seq // r, 3 * d_attn), BF16) for r in dilations],
        scratch_shapes=[pltpu.VMEM((tm, d), BF16)] + stage_shapes,
        compiler_params=_params(("arbitrary", "arbitrary"), vmem),
        name="in_proj",
    )(x2d, mod3, mod3, g_pre.reshape(1, d), w_in_bf16)


def _attn_kernel(slopes_ref, q_ref, k_ref, v_ref, o_ref, st_ref, bias_ref, *,
                 seq_len, dilation, half_window, tq, tk, bb, unroll):
    _, lq, width = q_ref.shape
    n_pairs = width // LANES
    n_qt = lq // tq
    n_var = bias_ref.shape[0]
    max_ks = seq_len - tk
    first = (pl.program_id(0) == 0) & (pl.program_id(1) == 0)

    @pl.when(first)
    def _():
        qi = lax.broadcasted_iota(jnp.int32, (tq, tk), 0)
        ki = lax.broadcasted_iota(jnp.int32, (tq, tk), 1)
        for var in range(n_var):
            delta = (0, half_window, tk - tq)[var]
            off = jnp.abs(ki - qi - delta)
            valid = off <= half_window
            dist = (dilation * off).astype(F32)
            for h in range(2 * n_pairs):
                bias = jnp.where(valid, (-LOG2_E * slopes_ref[h]) * dist, NEG_INF)
                r0 = (h % 2) * tq
                bias_ref[var, h // 2, r0:r0 + tq, :] = bias

    lane = lax.broadcasted_iota(jnp.int32, (1, LANES), 1)
    even_head = lane < HEAD_DIM
    keep_even = even_head.astype(BF16)
    keep_odd = 1.0 - keep_even
    q_tile0 = pl.program_id(1) * n_qt

    def tile(b, jt):
        qs_local = jt * tq if isinstance(jt, int) else pl.multiple_of(jt * tq, tq)
        qs = (q_tile0 + jt) * tq
        ks = pl.multiple_of(jnp.clip(qs - half_window, 0, max_ks), half_window)
        if n_var == 1:
            var = 0
        else:
            var = jnp.where(qs == 0, 0, jnp.where(qs == seq_len - tq, 2, 1))
        if dilation == 1:
            out_rows = pl.ds(qs_local, tq)
        else:
            res = (pl.program_id(0) * bb + b) % dilation
            out_rows = pl.ds(res + dilation * qs, tq, stride=dilation)
        m_tile = jnp.zeros((tq, LANES), F32)
        d_tile = jnp.ones((tq, LANES), F32)
        for p in range(n_pairs):
            cols = slice(p * LANES, (p + 1) * LANES)
            q2 = q_ref[b, pl.ds(qs_local, tq), cols]
            kw = k_ref[b, pl.ds(ks, tk), cols]
            vw = v_ref[b, pl.ds(ks, tk), cols]
            qq = jnp.concatenate([q2 * keep_even, q2 * keep_odd], axis=0)
            s = lax.dot_general(qq, kw, (((1,), (1,)), ((), ())), preferred_element_type=F32)
            s = s + bias_ref[var, p]
            m = jnp.max(s, axis=-1, keepdims=True)
            e = jnp.exp2(s - m)
            den = jnp.sum(e, axis=-1, keepdims=True)
            pv = jnp.dot(e.astype(BF16), vw, preferred_element_type=F32)
            o_ref[p, out_rows, :] = jnp.where(even_head, pv[:tq], pv[tq:])
            m_tile = jnp.where(lane == 2 * p, m[:tq], jnp.where(lane == 2 * p + 1, m[tq:], m_tile))
            d_tile = jnp.where(lane == 2 * p, den[:tq], jnp.where(lane == 2 * p + 1, den[tq:], d_tile))
        st_ref[0, out_rows, :] = m_tile
        st_ref[1, out_rows, :] = d_tile

    n_tiles = bb * n_qt
    group = min(unroll, n_tiles)

    def tile_group(it, carry):
        for u in range(group):
            idx = it * group + u
            tile(*((idx, 0) if n_qt == 1 else (idx // n_qt, idx % n_qt)))
        return carry

    if n_tiles == group:
        tile_group(0, 0)
    else:
        lax.fori_loop(0, n_tiles // group, tile_group, 0)


def _attention_branch(qkv, slopes, q_block, d_attn, window, dilation):
    n_seq, seq_len, _ = qkv.shape
    half_window = (window // 2) // dilation
    tq = 2 * half_window
    tk = min(2 * tq, seq_len)
    lq = min(seq_len, 512)
    bb = max(1, min(n_seq, 1024 // seq_len))
    n_var = 1 if tk == seq_len else 3
    n_pairs = d_attn // LANES
    bsz = n_seq // dilation
    seq = seq_len * dilation
    if dilation == 1:
        assert bb == 1
        out_rows = lq

        def out_index(b, t):
            return (b, 0, t, 0)
    else:
        assert lq == seq_len and dilation % bb == 0
        out_rows = seq

        def out_index(b, t):
            return ((b * bb) // dilation, 0, 0, 0)

    kern = functools.partial(_attn_kernel, seq_len=seq_len, dilation=dilation, half_window=half_window,
                             tq=tq, tk=tk, bb=bb, unroll=4)
    vmem = (2 * bb * (lq + 2 * seq_len) * d_attn * 2 + 2 * out_rows * (d_attn + 2 * LANES) * 4
            + n_var * n_pairs * 2 * tq * tk * 4 + (12 << 20))
    return pl.pallas_call(
        kern,
        grid=(n_seq // bb, seq_len // lq),
        in_specs=[pl.BlockSpec(memory_space=pltpu.SMEM),
                  pl.BlockSpec((bb, lq, d_attn), lambda b, t: (b, t, q_block)),
                  pl.BlockSpec((bb, seq_len, d_attn), lambda b, t: (b, 0, q_block + 1)),
                  pl.BlockSpec((bb, seq_len, d_attn), lambda b, t: (b, 0, q_block + 2))],
        out_specs=[pl.BlockSpec((None, n_pairs, out_rows, LANES), out_index),
                   pl.BlockSpec((None, 2, out_rows, LANES), out_index)],
        out_shape=[jax.ShapeDtypeStruct((bsz, n_pairs, seq, LANES), F32),
                   jax.ShapeDtypeStruct((bsz, 2, seq, LANES), F32)],
        scratch_shapes=[pltpu.VMEM((n_var, n_pairs, 2 * tq, tk), F32)],
        compiler_params=_params(("arbitrary", "arbitrary"), vmem),
        name=f"dilated_attn_r{dilation}",
    )(slopes, qkv, qkv, qkv)


def _mix_out_kernel(u_ref, bg_ref, cg_ref, zc_ref, za_ref, up_ref, cp_ref, un_ref, cn_ref,
                    cw_ref, cb_ref, gc_ref, ga_ref, w_ref, x_ref, gate_ref, gp_ref,
                    *rest, tiles_per_seq, n_chunks, n_branches):
    o_refs, st_refs = rest[:n_branches], rest[n_branches:2 * n_branches]
    out_ref, expand_ref, y_ref, wn_ref = rest[2 * n_branches:]
    tr, dc = u_ref.shape
    n_pairs = o_refs[0].shape[0]
    rc = tr // n_chunks
    i = pl.program_id(0)
    t_in_seq = i % tiles_per_seq

    @pl.when(i == 0)
    def _():
        k = lax.broadcasted_iota(jnp.int32, expand_ref.shape, 0)
        c = lax.broadcasted_iota(jnp.int32, expand_ref.shape, 1)
        expand_ref[...] = jnp.where(k % LANES == c // HEAD_DIM, 1.0, 0.0).astype(expand_ref.dtype)
        for r0 in range(0, w_ref.shape[0], 256):
            wn_ref[r0:r0 + 256, :] = w_ref[r0:r0 + 256, :].astype(wn_ref.dtype)

    ms = [ref[0] for ref in st_refs]
    ds = [ref[1] for ref in st_refs]
    mmax = functools.reduce(jnp.maximum, ms)
    es = [jnp.exp2(m - mmax) for m in ms]
    inv = 1.0 / functools.reduce(jnp.add, [e * d for e, d in zip(es, ds)])
    weights = []
    for e in es:
        w = e * inv
        w_hi = w.astype(BF16)
        w_lo = (w - w_hi.astype(F32)).astype(BF16)
        weights.append(jnp.dot(jnp.concatenate([w_hi, w_lo], axis=1), expand_ref[...],
                               preferred_element_type=F32))

    last = BF16_SUBLANES - 1
    v_before = cp_ref[last:last + 1, :].astype(F32) * up_ref[last:last + 1, :].astype(F32)
    v_after = cn_ref[0:1, :].astype(F32) * un_ref[0:1, :].astype(F32)
    v_before = jnp.where(t_in_seq == 0, 0.0, v_before)
    v_after = jnp.where(t_in_seq == tiles_per_seq - 1, 0.0, v_after)
    vs = [cg_ref[k * rc:(k + 1) * rc, :].astype(F32) * u_ref[k * rc:(k + 1) * rc, :].astype(F32)
          for k in range(n_chunks)]
    row = lax.broadcasted_iota(jnp.int32, (rc, 1), 0)
    cw = cw_ref[...]

    for k in range(n_chunks):
        rows = slice(k * rc, (k + 1) * rc)
        above = v_before if k == 0 else vs[k - 1][rc - 1:rc, :]
        below = v_after if k == n_chunks - 1 else vs[k + 1][0:1, :]
        v_prev = jnp.where(row == 0, above, pltpu.roll(vs[k], 1, axis=0))
        v_next = jnp.where(row == rc - 1, below, pltpu.roll(vs[k], rc - 1, axis=0))
        conv = cw[0:1, :] * v_prev + cw[1:2, :] * vs[k] + cw[2:3, :] * v_next + cb_ref[...]
        yc = bg_ref[rows, :].astype(F32) * conv
        yc = yc * lax.rsqrt(jnp.mean(yc * yc, axis=-1, keepdims=True) + EPS) * gc_ref[...]
        y_ref[rows, 0:dc] = (yc * _silu(zc_ref[rows, :].astype(F32))).astype(y_ref.dtype)

        branch_out = [jnp.concatenate([ref[p, rows, :] for p in range(n_pairs)], axis=1) for ref in o_refs]
        om = functools.reduce(jnp.add, [wb[rows, :] * o for wb, o in zip(weights, branch_out)])
        ya = om * lax.rsqrt(jnp.mean(om * om, axis=-1, keepdims=True) + EPS) * ga_ref[...]
        y_ref[rows, dc:] = (ya * _silu(za_ref[rows, :].astype(F32))).astype(y_ref.dtype)

        y = jnp.dot(y_ref[rows, :], wn_ref[...], preferred_element_type=F32)
        yn = y * lax.rsqrt(jnp.mean(y * y, axis=-1, keepdims=True) + EPS) * gp_ref[...]
        out_ref[rows, :] = x_ref[rows, :] + gate_ref[...] * yn


def _mix_out(proj, outs, stats, conv_w, conv_b, g_conv, g_attn, w_out, x2d, mod3, g_post,
             seq, d_conv, d_attn):
    m, d = x2d.shape
    tr = 256
    tps = seq // tr
    halo = BF16_SUBLANES
    hb = tr // halo
    n_halo_blocks = m // halo
    n_pairs = d_attn // LANES
    u_c, b_c, c_c, zc_c = 0, 1, 2, 3
    za_c = (4 * d_conv + 3 * d_attn) // d_attn

    def main(col, width=d_conv):
        return pl.BlockSpec((tr, width), lambda i: (i, col))

    def before(col):
        return pl.BlockSpec((halo, d_conv), lambda i: (jnp.maximum(i * hb - 1, 0), col))

    def after(col):
        return pl.BlockSpec((halo, d_conv), lambda i: (jnp.minimum((i + 1) * hb, n_halo_blocks - 1), col))

    def row_vec(n, rows=1):
        return pl.BlockSpec((rows, n), lambda i: (0, 0))

    def slabs(n):
        return pl.BlockSpec((None, n, tr, LANES), lambda i: (i // tps, 0, i % tps, 0))

    nb = len(outs)
    kern = functools.partial(_mix_out_kernel, tiles_per_seq=tps, n_chunks=2, n_branches=nb)
    vmem = (2 * (5 * tr * d_conv * 2 + nb * tr * (d_attn + 2 * LANES) * 4 + 2 * tr * d * 4)
            + (d_conv + d_attn) * d * (4 + 2) + 2 * LANES * d_attn * 2 + tr * (d_conv + d_attn) * 2 + (16 << 20))
    return pl.pallas_call(
        kern,
        grid=(m // tr,),
        in_specs=[main(u_c), main(b_c), main(c_c), main(zc_c), main(za_c, d_attn),
                  before(u_c), before(c_c), after(u_c), after(c_c),
                  row_vec(d_conv, CONV_WIDTH), row_vec(d_conv), row_vec(d_conv), row_vec(d_attn),
                  pl.BlockSpec((d_conv + d_attn, d), lambda i: (0, 0), pipeline_mode=pl.Buffered(1)),
                  pl.BlockSpec((tr, d), lambda i: (i, 0)),
                  pl.BlockSpec((None, 1, d), lambda i: (i // tps, 0, 2)),
                  row_vec(d)]
        + [slabs(n_pairs)] * nb + [slabs(2)] * nb,
        out_specs=pl.BlockSpec((tr, d), lambda i: (i, 0)),
        out_shape=jax.ShapeDtypeStruct((m, d), F32),
        scratch_shapes=[pltpu.VMEM((2 * LANES, d_attn), BF16), pltpu.VMEM((tr, d_conv + d_attn), BF16),
                        pltpu.VMEM((d_conv + d_attn, d), BF16)],
        compiler_params=_params(("arbitrary",), vmem),
        name="mix_out_proj",
    )(proj, proj, proj, proj, proj, proj, proj, proj, proj,
      conv_w, conv_b.reshape(1, d_conv), g_conv.reshape(1, d_conv), g_attn.reshape(1, d_attn),
      w_out, x2d, mod3, g_post.reshape(1, d), *outs, *stats)


def _layer(x, mod, g_pre, w_in, conv_w, conv_b, g_conv, g_attn, w_out, g_post):
    bsz, seq, d = x.shape
    d_conv = conv_w.shape[1]
    d_attn = g_attn.shape[0]
    n_proj = w_in.shape[1]
    n_heads = d_attn // HEAD_DIM
    q_col = 4 * d_conv
    assert d_attn % LANES == 0 and d_conv == d_attn and n_proj == 4 * d_conv + 4 * d_attn
    windows = {r: w for w, r in DILATED_BRANCHES}
    dilations = tuple(r for _, r in DILATED_BRANCHES if r > 1)
    assert set(windows) == {1, *dilations}

    x2d = x.reshape(bsz * seq, d)
    mod3 = mod.reshape(bsz, 1, 3 * d)
    proj, *qkv_d = _in_projection(x2d, mod3, g_pre, w_in.astype(BF16), bsz, seq, q_col, d_attn, dilations)

    slopes = 2.0 ** (-8.0 * jnp.arange(1, n_heads + 1, dtype=F32) / n_heads)
    branches = [_attention_branch(proj.reshape(bsz, seq, n_proj), slopes, q_col // d_attn, d_attn, windows[1], 1)]
    for r, qkv in zip(dilations, qkv_d):
        branches.append(_attention_branch(qkv.reshape(bsz * r, seq // r, 3 * d_attn), slopes, 0, d_attn,
                                          windows[r], r))
    outs, stats = zip(*branches)

    out = _mix_out(proj, outs, stats, conv_w, conv_b, g_conv, g_attn, w_out, x2d, mod3, g_post,
                   seq, d_conv, d_attn)
    return out.reshape(bsz, seq, d)


def kernel(x, c, w_ada, b_ada, g_pre, w_in, conv_w, conv_b, g_conv, g_attn, w_out, g_post):
    for layer in range(w_ada.shape[0]):
        mod = _modulation(c, w_ada[layer], b_ada[layer])
        x = _layer(x, mod, g_pre[layer], w_in[layer], conv_w[layer], conv_b[layer],
                   g_conv[layer], g_attn[layer], w_out[layer], g_post[layer])
    return x
```

```python
import functools

import jax
import jax.numpy as jnp
from jax import lax
from jax.experimental import pallas as pl
from jax.experimental.pallas import tpu as pltpu

HEAD_DIM = 64
CONV_WIDTH = 3
DILATED_BRANCHES = ((128, 1), (512, 4), (2048, 16))
EPS = 1e-6
NEG_INF = -1e30
LOG2_E = 1.4426950408889634

LANES = 128
BF16_SUBLANES = 16
V7X_VMEM_LIMIT = 56 * 1024 * 1024

F32 = jnp.float32
BF16 = jnp.bfloat16


def _silu(v):
    half = 0.5 * v
    return half + half * jnp.tanh(half)


def _params(semantics, vmem_bytes):
    return pltpu.CompilerParams(dimension_semantics=semantics,
                                vmem_limit_bytes=min(int(vmem_bytes), V7X_VMEM_LIMIT))


def _mod_kernel(c_ref, w_ref, b_ref, o_ref):
    c_act = _silu(c_ref[...]).astype(BF16)
    w = w_ref[...].astype(BF16)
    o_ref[...] = jnp.dot(c_act, w, preferred_element_type=F32) + b_ref[...]


def _modulation(c, w_ada, b_ada):
    bsz, d = c.shape
    n = w_ada.shape[1]
    tn = 512
    return pl.pallas_call(
        _mod_kernel,
        grid=(n // tn,),
        in_specs=[pl.BlockSpec((bsz, d), lambda j: (0, 0)),
                  pl.BlockSpec((d, tn), lambda j: (0, j)),
                  pl.BlockSpec((1, tn), lambda j: (0, j))],
        out_specs=pl.BlockSpec((bsz, tn), lambda j: (0, j)),
        out_shape=jax.ShapeDtypeStruct((bsz, n), F32),
        compiler_params=_params(("arbitrary",), 4 * d * tn * 4),
        name="adaln_mod",
    )(c, w_ada, b_ada.reshape(1, n))


def _inproj_kernel(x_ref, shift_ref, scale_ref, g_ref, w_ref, o_ref, *rest,
                   dilations, q_block, q_scale, chunk):
    res_refs, (h_ref, stage_refs) = rest[:len(dilations)], (rest[len(dilations)], rest[len(dilations) + 1:])
    tm, tn = o_ref.shape
    n_chunks = tm // chunk
    n_slabs = tn // LANES
    j = pl.program_id(1)
    is_qkv = (j >= q_block) & (j < q_block + 3)

    def rows_of(ck):
        return slice(ck * chunk, (ck + 1) * chunk)

    @pl.when(j == 0)
    def _():
        g = g_ref[...]
        one_plus_scale = 1.0 + scale_ref[...]
        shift = shift_ref[...]
        for ck in range(n_chunks):
            xc = x_ref[rows_of(ck), :]
            ms = jnp.mean(xc * xc, axis=-1, keepdims=True)
            y = xc * lax.rsqrt(ms + EPS) * g
            h_ref[rows_of(ck), :] = (y * one_plus_scale + shift).astype(BF16)
            acc = jnp.dot(h_ref[rows_of(ck), :], w_ref[...], preferred_element_type=F32)
            o_ref[rows_of(ck), :] = acc.astype(o_ref.dtype)

    @pl.when((j != 0) & jnp.logical_not(is_qkv))
    def _():
        acc = jnp.dot(h_ref[...], w_ref[...], preferred_element_type=F32)
        o_ref[...] = acc.astype(o_ref.dtype)

    @pl.when(is_qkv)
    def _():
        fac = jnp.where(j == q_block, q_scale, 1.0).astype(F32)
        for ck in range(n_chunks):
            acc = jnp.dot(h_ref[rows_of(ck), :], w_ref[...], preferred_element_type=F32) * fac
            o_ref[rows_of(ck), :] = acc.astype(o_ref.dtype)
            slot = ck % 2
            for c in range(n_slabs):
                stage_refs[0][slot, c] = acc[:, c * LANES:(c + 1) * LANES]
            for c in range(n_slabs):
                lanes = slice(c * LANES, (c + 1) * LANES)
                classes = {0: (stage_refs[0].at[slot, c], chunk)}
                r_prev = 1
                for lvl, (r, res_ref) in enumerate(zip(dilations, res_refs)):
                    step = r // r_prev
                    new_classes = {}
                    for res_prev, (src, n_rows) in classes.items():
                        for k in range(step):
                            res = res_prev + r_prev * k
                            n_out = n_rows // step
                            rows = src[pl.ds(k, n_out, stride=step), :]
                            res_ref[res, ck * n_out:(ck + 1) * n_out, lanes] = rows.astype(res_ref.dtype)
                            if lvl + 1 < len(dilations):
                                dst = stage_refs[lvl + 1].at[slot, c, res]
                                dst[...] = rows
                                new_classes[res] = (dst, n_out)
                    classes, r_prev = new_classes, r


def _in_projection(x2d, mod3, g_pre, w_in_bf16, bsz, seq, q_col, d_attn, dilations):
    m, d = x2d.shape
    n = w_in_bf16.shape[1]
    tm, tn = 1024, d_attn
    tps = seq // tm
    q_block = q_col // tn
    chunk = 256
    assert q_block > 0 and all(b % a == 0 for a, b in zip((1,) + dilations, dilations))
    kern = functools.partial(_inproj_kernel, dilations=dilations, q_block=q_block,
                             q_scale=HEAD_DIM ** -0.5 * LOG2_E, chunk=chunk)
    stage_shapes = [pltpu.VMEM((2, tn // LANES, chunk, LANES), F32)]
    stage_shapes += [pltpu.VMEM((2, tn // LANES, r, chunk // r, LANES), F32) for r in dilations[:-1]]

    def res_spec(r):
        return pl.BlockSpec((None, r, tm // r, tn),
                            lambda i, j: (i // tps, 0, i % tps, jnp.clip(j - q_block, 0, 2)))

    vmem = (2 * tm * d * 4 + 2 * d * tn * 2 + 2 * (1 + len(dilations)) * tm * tn * 2
            + tm * d * 2 + len(dilations) * 2 * chunk * tn * 4 + tm * tn * 4 + (10 << 20))
    return pl.pallas_call(
        kern,
        grid=(m // tm, n // tn),
        in_specs=[pl.BlockSpec((tm, d), lambda i, j: (i, 0)),
                  pl.BlockSpec((None, 1, d), lambda i, j: (i // tps, 0, 0)),
                  pl.BlockSpec((None, 1, d), lambda i, j: (i // tps, 0, 1)),
                  pl.BlockSpec((1, d), lambda i, j: (0, 0)),
                  pl.BlockSpec((d, tn), lambda i, j: (0, j))],
        out_specs=[pl.BlockSpec((tm, tn), lambda i, j: (i, j))] + [res_spec(r) for r in dilations],
        out_shape=[jax.ShapeDtypeStruct((m, n), BF16)]
        + [jax.ShapeDtypeStruct((bsz, r, seq // r, 3 * d_attn), BF16) for r in dilations],
        scratch_shapes=[pltpu.VMEM((tm, d), BF16)] + stage_shapes,
        compiler_params=_params(("arbitrary", "arbitrary"), vmem),
        name="in_proj",
    )(x2d, mod3, mod3, g_pre.reshape(1, d), w_in_bf16)


def _attn_kernel(slopes_ref, *refs, seq_len, dilation, half_window, tq, tk, bb, width, unroll):
    *in_refs, o_ref, bias_ref = refs
    if len(in_refs) == 1:
        q_ref = k_ref = v_ref = in_refs[0]
        k_col, v_col = width, 2 * width
    else:
        q_ref, k_ref, v_ref = in_refs
        k_col = v_col = 0
    lq = q_ref.shape[1]
    n_pairs = width // LANES
    n_qt = lq // tq
    n_var = bias_ref.shape[0]
    max_ks = seq_len - tk
    first = (pl.program_id(0) == 0) & (pl.program_id(1) == 0)

    @pl.when(first)
    def _():
        qi = lax.broadcasted_iota(jnp.int32, (tq, tk), 0)
        ki = lax.broadcasted_iota(jnp.int32, (tq, tk), 1)
        for var in range(n_var):
            delta = (0, half_window, tk - tq)[var]
            off = jnp.abs(ki - qi - delta)
            valid = off <= half_window
            dist = (dilation * off).astype(F32)
            for h in range(2 * n_pairs):
                bias = jnp.where(valid, (-LOG2_E * slopes_ref[h]) * dist, NEG_INF)
                r0 = (h % 2) * tq
                bias_ref[var, h // 2, r0:r0 + tq, :] = bias

    lane = lax.broadcasted_iota(jnp.int32, (1, LANES), 1)
    even_head = lane < HEAD_DIM
    keep_even = even_head.astype(BF16)
    keep_odd = 1.0 - keep_even
    q_tile0 = pl.program_id(1) * n_qt

    def tile(b, jt):
        qs_local = jt * tq if isinstance(jt, int) else pl.multiple_of(jt * tq, tq)
        qs = (q_tile0 + jt) * tq
        ks = pl.multiple_of(jnp.clip(qs - half_window, 0, max_ks), half_window)
        if n_var == 1:
            var = 0
        else:
            var = jnp.where(qs == 0, 0, jnp.where(qs == seq_len - tq, 2, 1))
        if dilation == 1:
            out_rows = pl.ds(qs_local, tq)
        else:
            res = (pl.program_id(0) * bb + b) % dilation
            out_rows = pl.ds(res + dilation * qs, tq, stride=dilation)
        m_tile = jnp.zeros((tq, LANES), F32)
        d_tile = jnp.ones((tq, LANES), F32)
        for p in range(n_pairs):
            cols = slice(p * LANES, (p + 1) * LANES)
            q2 = q_ref[b, pl.ds(qs_local, tq), cols]
            kw = k_ref[b, pl.ds(ks, tk), k_col + p * LANES:k_col + (p + 1) * LANES]
            vw = v_ref[b, pl.ds(ks, tk), v_col + p * LANES:v_col + (p + 1) * LANES]
            qq = jnp.concatenate([q2 * keep_even, q2 * keep_odd], axis=0)
            s = lax.dot_general(qq, kw, (((1,), (1,)), ((), ())), preferred_element_type=F32)
            s = s + bias_ref[var, p]
            m = jnp.max(s, axis=-1, keepdims=True)
            e = jnp.exp2(s - m)
            den = jnp.sum(e, axis=-1, keepdims=True)
            pv = jnp.dot(e.astype(BF16), vw, preferred_element_type=F32)
            o_ref[p, out_rows, :] = jnp.where(even_head, pv[:tq], pv[tq:])
            m_tile = jnp.where(lane == 2 * p, m[:tq], jnp.where(lane == 2 * p + 1, m[tq:], m_tile))
            d_tile = jnp.where(lane == 2 * p, den[:tq], jnp.where(lane == 2 * p + 1, den[tq:], d_tile))
        o_ref[n_pairs, out_rows, :] = m_tile
        o_ref[n_pairs + 1, out_rows, :] = d_tile

    n_tiles = bb * n_qt
    group = min(unroll, n_tiles)

    def tile_group(it, carry):
        for u in range(group):
            idx = it * group + u
            tile(*((idx, 0) if n_qt == 1 else (idx // n_qt, idx % n_qt)))
        return carry

    if n_tiles == group:
        tile_group(0, 0)
    else:
        lax.fori_loop(0, n_tiles // group, tile_group, 0)


def _attention_branch(qkv, slopes, q_block, d_attn, window, dilation):
    n_seq, seq_len, _ = qkv.shape
    half_window = (window // 2) // dilation
    tq = 2 * half_window
    tk = min(2 * tq, seq_len)
    lq = min(seq_len, 512)
    bb = max(1, min(n_seq, 1024 // seq_len))
    n_var = 1 if tk == seq_len else 3
    n_pairs = d_attn // LANES
    bsz = n_seq // dilation
    seq = seq_len * dilation
    if dilation == 1:
        assert bb == 1
        out_rows = lq

        def out_index(b, t):
            return (b, 0, t, 0)
    else:
        assert lq == seq_len and dilation % bb == 0
        out_rows = seq

        def out_index(b, t):
            return ((b * bb) // dilation, 0, 0, 0)

    kern = functools.partial(_attn_kernel, seq_len=seq_len, dilation=dilation, half_window=half_window,
                             tq=tq, tk=tk, bb=bb, width=d_attn, unroll=4)
    if lq == seq_len and qkv.shape[2] == 3 * d_attn:
        in_specs = [pl.BlockSpec((bb, seq_len, 3 * d_attn), lambda b, t: (b, 0, 0))]
    else:
        in_specs = [pl.BlockSpec((bb, lq, d_attn), lambda b, t: (b, t, q_block)),
                    pl.BlockSpec((bb, seq_len, d_attn), lambda b, t: (b, 0, q_block + 1)),
                    pl.BlockSpec((bb, seq_len, d_attn), lambda b, t: (b, 0, q_block + 2))]
    vmem = (2 * bb * (lq + 2 * seq_len) * d_attn * 2 + 2 * out_rows * (d_attn + 2 * LANES) * 4
            + n_var * n_pairs * 2 * tq * tk * 4 + (12 << 20))
    return pl.pallas_call(
        kern,
        grid=(n_seq // bb, seq_len // lq),
        in_specs=[pl.BlockSpec(memory_space=pltpu.SMEM)] + in_specs,
        out_specs=pl.BlockSpec((None, n_pairs + 2, out_rows, LANES), out_index),
        out_shape=jax.ShapeDtypeStruct((bsz, n_pairs + 2, seq, LANES), F32),
        scratch_shapes=[pltpu.VMEM((n_var, n_pairs, 2 * tq, tk), F32)],
        compiler_params=_params(("arbitrary", "arbitrary"), vmem),
        name=f"dilated_attn_r{dilation}",
    )(slopes, *([qkv] * len(in_specs)))


def _mix_out_kernel(conv_ref, za_ref, before_ref, after_ref, cw_ref, cb_ref, gc_ref, ga_ref,
                    w_ref, x_ref, gate_ref, gp_ref, *rest, tiles_per_seq, n_chunks, n_branches):
    o_refs = rest[:n_branches]
    out_ref, expand_ref, y_ref, wn_ref = rest[n_branches:]
    tr, dc = za_ref.shape
    n_pairs = o_refs[0].shape[0] - 2
    u_cols, bg_cols, cg_cols, zc_cols = (slice(n * dc, (n + 1) * dc) for n in range(4))
    rc = tr // n_chunks
    i = pl.program_id(0)
    t_in_seq = i % tiles_per_seq

    @pl.when(i == 0)
    def _():
        k = lax.broadcasted_iota(jnp.int32, expand_ref.shape, 0)
        c = lax.broadcasted_iota(jnp.int32, expand_ref.shape, 1)
        expand_ref[...] = jnp.where(k % LANES == c // HEAD_DIM, 1.0, 0.0).astype(expand_ref.dtype)
        for r0 in range(0, w_ref.shape[0], 256):
            wn_ref[r0:r0 + 256, :] = w_ref[r0:r0 + 256, :].astype(wn_ref.dtype)

    ms = [ref[n_pairs] for ref in o_refs]
    ds = [ref[n_pairs + 1] for ref in o_refs]
    mmax = functools.reduce(jnp.maximum, ms)
    es = [jnp.exp2(m - mmax) for m in ms]
    inv = 1.0 / functools.reduce(jnp.add, [e * d for e, d in zip(es, ds)])
    weights = []
    for e in es:
        w = e * inv
        w_hi = w.astype(BF16)
        w_lo = (w - w_hi.astype(F32)).astype(BF16)
        weights.append(jnp.dot(jnp.concatenate([w_hi, w_lo], axis=1), expand_ref[...],
                               preferred_element_type=F32))

    last = BF16_SUBLANES - 1
    v_before = before_ref[last:last + 1, cg_cols].astype(F32) * before_ref[last:last + 1, u_cols].astype(F32)
    v_after = after_ref[0:1, cg_cols].astype(F32) * after_ref[0:1, u_cols].astype(F32)
    v_before = jnp.where(t_in_seq == 0, 0.0, v_before)
    v_after = jnp.where(t_in_seq == tiles_per_seq - 1, 0.0, v_after)
    vs = [conv_ref[k * rc:(k + 1) * rc, cg_cols].astype(F32) * conv_ref[k * rc:(k + 1) * rc, u_cols].astype(F32)
          for k in range(n_chunks)]
    row = lax.broadcasted_iota(jnp.int32, (rc, 1), 0)
    cw = cw_ref[...]

    for k in range(n_chunks):
        rows = slice(k * rc, (k + 1) * rc)
        above = v_before if k == 0 else vs[k - 1][rc - 1:rc, :]
        below = v_after if k == n_chunks - 1 else vs[k + 1][0:1, :]
        v_prev = jnp.where(row == 0, above, pltpu.roll(vs[k], 1, axis=0))
        v_next = jnp.where(row == rc - 1, below, pltpu.roll(vs[k], rc - 1, axis=0))
        conv = cw[0:1, :] * v_prev + cw[1:2, :] * vs[k] + cw[2:3, :] * v_next + cb_ref[...]
        yc = conv_ref[rows, bg_cols].astype(F32) * conv
        yc = yc * lax.rsqrt(jnp.mean(yc * yc, axis=-1, keepdims=True) + EPS) * gc_ref[...]
        y_ref[rows, 0:dc] = (yc * _silu(conv_ref[rows, zc_cols].astype(F32))).astype(y_ref.dtype)

        branch_out = [jnp.concatenate([ref[p, rows, :] for p in range(n_pairs)], axis=1) for ref in o_refs]
        om = functools.reduce(jnp.add, [wb[rows, :] * o for wb, o in zip(weights, branch_out)])
        ya = om * lax.rsqrt(jnp.mean(om * om, axis=-1, keepdims=True) + EPS) * ga_ref[...]
        y_ref[rows, dc:] = (ya * _silu(za_ref[rows, :].astype(F32))).astype(y_ref.dtype)

        y = jnp.dot(y_ref[rows, :], wn_ref[...], preferred_element_type=F32)
        yn = y * lax.rsqrt(jnp.mean(y * y, axis=-1, keepdims=True) + EPS) * gp_ref[...]
        out_ref[rows, :] = x_ref[rows, :] + gate_ref[...] * yn


def _mix_out(proj, outs, conv_w, conv_b, g_conv, g_attn, w_out, x2d, mod3, g_post, seq, d_conv, d_attn):
    m, d = x2d.shape
    tr = 256
    tps = seq // tr
    halo = BF16_SUBLANES
    hb = tr // halo
    n_halo_blocks = m // halo
    n_pairs = d_attn // LANES
    za_c = (4 * d_conv + 3 * d_attn) // d_attn

    def row_vec(n, rows=1):
        return pl.BlockSpec((rows, n), lambda i: (0, 0))

    branch = pl.BlockSpec((None, n_pairs + 2, tr, LANES), lambda i: (i // tps, 0, i % tps, 0))

    nb = len(outs)
    kern = functools.partial(_mix_out_kernel, tiles_per_seq=tps, n_chunks=2, n_branches=nb)
    vmem = (2 * ((4 * tr + 2 * halo) * d_conv * 2 + tr * d_attn * 2 + nb * tr * (d_attn + 2 * LANES) * 4
                 + 2 * tr * d * 4)
            + (d_conv + d_attn) * d * (4 + 2) + 2 * LANES * d_attn * 2 + tr * (d_conv + d_attn) * 2 + (16 << 20))
    return pl.pallas_call(
        kern,
        grid=(m // tr,),
        in_specs=[pl.BlockSpec((tr, 4 * d_conv), lambda i: (i, 0)),
                  pl.BlockSpec((tr, d_attn), lambda i: (i, za_c)),
                  pl.BlockSpec((halo, 4 * d_conv), lambda i: (jnp.maximum(i * hb - 1, 0), 0)),
                  pl.BlockSpec((halo, 4 * d_conv), lambda i: (jnp.minimum((i + 1) * hb, n_halo_blocks - 1), 0)),
                  row_vec(d_conv, CONV_WIDTH), row_vec(d_conv), row_vec(d_conv), row_vec(d_attn),
                  pl.BlockSpec((d_conv + d_attn, d), lambda i: (0, 0), pipeline_mode=pl.Buffered(1)),
                  pl.BlockSpec((tr, d), lambda i: (i, 0)),
                  pl.BlockSpec((None, 1, d), lambda i: (i // tps, 0, 2)),
                  row_vec(d)]
        + [branch] * nb,
        out_specs=pl.BlockSpec((tr, d), lambda i: (i, 0)),
        out_shape=jax.ShapeDtypeStruct((m, d), F32),
        scratch_shapes=[pltpu.VMEM((2 * LANES, d_attn), BF16), pltpu.VMEM((tr, d_conv + d_attn), BF16),
                        pltpu.VMEM((d_conv + d_attn, d), BF16)],
        compiler_params=_params(("arbitrary",), vmem),
        name="mix_out_proj",
    )(proj, proj, proj, proj,
      conv_w, conv_b.reshape(1, d_conv), g_conv.reshape(1, d_conv), g_attn.reshape(1, d_attn),
      w_out, x2d, mod3, g_post.reshape(1, d), *outs)


def _layer(x, mod, g_pre, w_in, conv_w, conv_b, g_conv, g_attn, w_out, g_post):
    bsz, seq, d = x.shape
    d_conv = conv_w.shape[1]
    d_attn = g_attn.shape[0]
    n_proj = w_in.shape[1]
    n_heads = d_attn // HEAD_DIM
    q_col = 4 * d_conv
    assert d_attn % LANES == 0 and d_conv == d_attn and n_proj == 4 * d_conv + 4 * d_attn
    windows = {r: w for w, r in DILATED_BRANCHES}
    dilations = tuple(r for _, r in DILATED_BRANCHES if r > 1)
    assert set(windows) == {1, *dilations}

    x2d = x.reshape(bsz * seq, d)
    mod3 = mod.reshape(bsz, 1, 3 * d)
    proj, *qkv_d = _in_projection(x2d, mod3, g_pre, w_in.astype(BF16), bsz, seq, q_col, d_attn, dilations)

    slopes = 2.0 ** (-8.0 * jnp.arange(1, n_heads + 1, dtype=F32) / n_heads)
    outs = [_attention_branch(proj.reshape(bsz, seq, n_proj), slopes, q_col // d_attn, d_attn, windows[1], 1)]
    for r, qkv in zip(dilations, qkv_d):
        outs.append(_attention_branch(qkv.reshape(bsz * r, seq // r, 3 * d_attn), slopes, 0, d_attn, windows[r], r))

    out = _mix_out(proj, outs, conv_w, conv_b, g_conv, g_attn, w_out, x2d, mod3, g_post, seq, d_conv, d_attn)
    return out.reshape(bsz, seq, d)


def kernel(x, c, w_ada, b_ada, g_pre, w_in, conv_w, conv_b, g_conv, g_attn, w_out, g_post):
    for layer in range(w_ada.shape[0]):
        mod = _modulation(c, w_ada[layer], b_ada[layer])
        x = _layer(x, mod, g_pre[layer], w_in[layer], conv_w[layer], conv_b[layer],
                   g_conv[layer], g_attn[layer], w_out[layer], g_post[layer])
    return x
```

```python
import functools

import jax
import jax.numpy as jnp
from jax import lax
from jax.experimental import pallas as pl
from jax.experimental.pallas import tpu as pltpu

HEAD_DIM = 64
CONV_WIDTH = 3
DILATED_BRANCHES = ((128, 1), (512, 4), (2048, 16))
EPS = 1e-6
NEG_INF = -1e30
LOG2_E = 1.4426950408889634

LANES = 128
BF16_SUBLANES = 16
V7X_VMEM_LIMIT = 56 * 1024 * 1024

F32 = jnp.float32
BF16 = jnp.bfloat16


def _silu(v):
    half = 0.5 * v
    return half + half * jnp.tanh(half)


def _params(semantics, vmem_bytes):
    return pltpu.CompilerParams(dimension_semantics=semantics,
                                vmem_limit_bytes=min(int(vmem_bytes), V7X_VMEM_LIMIT))


def _mod_kernel(c_ref, w_ref, b_ref, o_ref):
    c_act = _silu(c_ref[...]).astype(BF16)
    w = w_ref[...].astype(BF16)
    o_ref[...] = jnp.dot(c_act, w, preferred_element_type=F32) + b_ref[...]


def _modulation(c, w_ada, b_ada):
    bsz, d = c.shape
    n = w_ada.shape[1]
    tn = 512
    return pl.pallas_call(
        _mod_kernel,
        grid=(n // tn,),
        in_specs=[pl.BlockSpec((bsz, d), lambda j: (0, 0)),
                  pl.BlockSpec((d, tn), lambda j: (0, j)),
                  pl.BlockSpec((1, tn), lambda j: (0, j))],
        out_specs=pl.BlockSpec((bsz, tn), lambda j: (0, j)),
        out_shape=jax.ShapeDtypeStruct((bsz, n), F32),
        compiler_params=_params(("arbitrary",), 4 * d * tn * 4),
        name="adaln_mod",
    )(c, w_ada, b_ada.reshape(1, n))


def _inproj_kernel(x_ref, shift_ref, scale_ref, g_ref, w_ref, o_ref, *rest,
                   dilations, q_block, q_scale, chunk):
    res_refs, (h_ref, stage_refs) = rest[:len(dilations)], (rest[len(dilations)], rest[len(dilations) + 1:])
    tm, tn = o_ref.shape
    n_chunks = tm // chunk
    n_slabs = tn // LANES
    j = pl.program_id(1)
    is_qkv = (j >= q_block) & (j < q_block + 3)

    def rows_of(ck):
        return slice(ck * chunk, (ck + 1) * chunk)

    @pl.when(j == 0)
    def _():
        g = g_ref[...]
        one_plus_scale = 1.0 + scale_ref[...]
        shift = shift_ref[...]
        for ck in range(n_chunks):
            xc = x_ref[rows_of(ck), :]
            ms = jnp.mean(xc * xc, axis=-1, keepdims=True)
            y = xc * lax.rsqrt(ms + EPS) * g
            h_ref[rows_of(ck), :] = (y * one_plus_scale + shift).astype(BF16)
            acc = jnp.dot(h_ref[rows_of(ck), :], w_ref[...], preferred_element_type=F32)
            o_ref[rows_of(ck), :] = acc.astype(o_ref.dtype)

    @pl.when((j != 0) & jnp.logical_not(is_qkv))
    def _():
        acc = jnp.dot(h_ref[...], w_ref[...], preferred_element_type=F32)
        o_ref[...] = acc.astype(o_ref.dtype)

    @pl.when(is_qkv)
    def _():
        fac = jnp.where(j == q_block, q_scale, 1.0).astype(F32)
        for ck in range(n_chunks):
            acc = jnp.dot(h_ref[rows_of(ck), :], w_ref[...], preferred_element_type=F32) * fac
            o_ref[rows_of(ck), :] = acc.astype(o_ref.dtype)
            slot = ck % 2
            for c in range(n_slabs):
                stage_refs[0][slot, c] = acc[:, c * LANES:(c + 1) * LANES]
            for c in range(n_slabs):
                lanes = slice(c * LANES, (c + 1) * LANES)
                classes = {0: (stage_refs[0].at[slot, c], chunk)}
                r_prev = 1
                for lvl, (r, res_ref) in enumerate(zip(dilations, res_refs)):
                    step = r // r_prev
                    new_classes = {}
                    for res_prev, (src, n_rows) in classes.items():
                        for k in range(step):
                            res = res_prev + r_prev * k
                            n_out = n_rows // step
                            rows = src[pl.ds(k, n_out, stride=step), :]
                            res_ref[res, ck * n_out:(ck + 1) * n_out, lanes] = rows.astype(res_ref.dtype)
                            if lvl + 1 < len(dilations):
                                dst = stage_refs[lvl + 1].at[slot, c, res]
                                dst[...] = rows
                                new_classes[res] = (dst, n_out)
                    classes, r_prev = new_classes, r


def _in_projection(x2d, mod3, g_pre, w_in_bf16, bsz, seq, q_col, d_attn, dilations):
    m, d = x2d.shape
    n = w_in_bf16.shape[1]
    tm, tn = 1024, d_attn
    tps = seq // tm
    q_block = q_col // tn
    chunk = 256
    assert q_block > 0 and all(b % a == 0 for a, b in zip((1,) + dilations, dilations))
    kern = functools.partial(_inproj_kernel, dilations=dilations, q_block=q_block,
                             q_scale=HEAD_DIM ** -0.5 * LOG2_E, chunk=chunk)
    stage_shapes = [pltpu.VMEM((2, tn // LANES, chunk, LANES), F32)]
    stage_shapes += [pltpu.VMEM((2, tn // LANES, r, chunk // r, LANES), F32) for r in dilations[:-1]]

    def res_spec(r):
        return pl.BlockSpec((None, r, tm // r, tn),
                            lambda i, j: (i // tps, 0, i % tps, jnp.clip(j - q_block, 0, 2)))

    vmem = (2 * tm * d * 4 + 2 * d * tn * 2 + 2 * (1 + len(dilations)) * tm * tn * 2
            + tm * d * 2 + len(dilations) * 2 * chunk * tn * 4 + tm * tn * 4 + (10 << 20))
    return pl.pallas_call(
        kern,
        grid=(m // tm, n // tn),
        in_specs=[pl.BlockSpec((tm, d), lambda i, j: (i, 0)),
                  pl.BlockSpec((None, 1, d), lambda i, j: (i // tps, 0, 0)),
                  pl.BlockSpec((None, 1, d), lambda i, j: (i // tps, 0, 1)),
                  pl.BlockSpec((1, d), lambda i, j: (0, 0)),
                  pl.BlockSpec((d, tn), lambda i, j: (0, j))],
        out_specs=[pl.BlockSpec((tm, tn), lambda i, j: (i, j))] + [res_spec(r) for r in dilations],
        out_shape=[jax.ShapeDtypeStruct((m, n), BF16)]
        + [jax.ShapeDtypeStruct((bsz, r, seq // r, 3 * d_attn), BF16) for r in dilations],
        scratch_shapes=[pltpu.VMEM((tm, d), BF16)] + stage_shapes,
        compiler_params=_params(("arbitrary", "arbitrary"), vmem),
        name="in_proj",
    )(x2d, mod3, mod3, g_pre.reshape(1, d), w_in_bf16)


def _attn_kernel(slopes_ref, *refs, seq_len, dilation, half_window, tq, tk, bb, width, unroll, mxu_rowsum):
    *in_refs, o_ref, bias_ref = refs
    if len(in_refs) == 1:
        q_ref = k_ref = v_ref = in_refs[0]
        k_col, v_col = width, 2 * width
    else:
        q_ref, k_ref, v_ref = in_refs
        k_col = v_col = 0
    lq = q_ref.shape[1]
    n_pairs = width // LANES
    n_qt = lq // tq
    n_var = bias_ref.shape[0]
    max_ks = seq_len - tk
    first = (pl.program_id(0) == 0) & (pl.program_id(1) == 0)

    @pl.when(first)
    def _():
        qi = lax.broadcasted_iota(jnp.int32, (tq, tk), 0)
        ki = lax.broadcasted_iota(jnp.int32, (tq, tk), 1)
        for var in range(n_var):
            delta = (0, half_window, tk - tq)[var]
            off = jnp.abs(ki - qi - delta)
            valid = off <= half_window
            dist = (dilation * off).astype(F32)
            for h in range(2 * n_pairs):
                bias = jnp.where(valid, (-LOG2_E * slopes_ref[h]) * dist, NEG_INF)
                r0 = (h % 2) * tq
                bias_ref[var, h // 2, r0:r0 + tq, :] = bias

    lane = lax.broadcasted_iota(jnp.int32, (1, LANES), 1)
    even_head = lane < HEAD_DIM
    keep_even = even_head.astype(BF16)
    keep_odd = 1.0 - keep_even
    ones = jnp.ones((tk, LANES), BF16)
    q_tile0 = pl.program_id(1) * n_qt

    def tile(b, jt):
        qs_local = jt * tq if isinstance(jt, int) else pl.multiple_of(jt * tq, tq)
        qs = (q_tile0 + jt) * tq
        ks = pl.multiple_of(jnp.clip(qs - half_window, 0, max_ks), half_window)
        if n_var == 1:
            var = 0
        else:
            var = jnp.where(qs == 0, 0, jnp.where(qs == seq_len - tq, 2, 1))
        if dilation == 1:
            out_rows = pl.ds(qs_local, tq)
        else:
            res = (pl.program_id(0) * bb + b) % dilation
            out_rows = pl.ds(res + dilation * qs, tq, stride=dilation)
        m_tile = jnp.zeros((tq, LANES), F32)
        d_tile = jnp.ones((tq, LANES), F32)
        for p in range(n_pairs):
            cols = slice(p * LANES, (p + 1) * LANES)
            q2 = q_ref[b, pl.ds(qs_local, tq), cols]
            kw = k_ref[b, pl.ds(ks, tk), k_col + p * LANES:k_col + (p + 1) * LANES]
            vw = v_ref[b, pl.ds(ks, tk), v_col + p * LANES:v_col + (p + 1) * LANES]
            qq = jnp.concatenate([q2 * keep_even, q2 * keep_odd], axis=0)
            s = lax.dot_general(qq, kw, (((1,), (1,)), ((), ())), preferred_element_type=F32)
            s = s + bias_ref[var, p]
            m = jnp.max(s, axis=-1, keepdims=True)
            e = jnp.exp2(s - m)
            if mxu_rowsum:
                pv = jnp.dot(e.astype(BF16), jnp.concatenate([vw, ones], axis=1), preferred_element_type=F32)
                pv, den = pv[:, :LANES], pv[:, LANES:]
            else:
                den = jnp.sum(e, axis=-1, keepdims=True)
                pv = jnp.dot(e.astype(BF16), vw, preferred_element_type=F32)
            o_ref[p, out_rows, :] = jnp.where(even_head, pv[:tq], pv[tq:])
            m_tile = jnp.where(lane == 2 * p, m[:tq], jnp.where(lane == 2 * p + 1, m[tq:], m_tile))
            d_tile = jnp.where(lane == 2 * p, den[:tq], jnp.where(lane == 2 * p + 1, den[tq:], d_tile))
        o_ref[n_pairs, out_rows, :] = m_tile
        o_ref[n_pairs + 1, out_rows, :] = d_tile

    n_tiles = bb * n_qt
    group = min(unroll, n_tiles)

    def tile_group(it, carry):
        for u in range(group):
            idx = it * group + u
            tile(*((idx, 0) if n_qt == 1 else (idx // n_qt, idx % n_qt)))
        return carry

    if n_tiles == group:
        tile_group(0, 0)
    else:
        lax.fori_loop(0, n_tiles // group, tile_group, 0)


def _attention_branch(qkv, slopes, q_block, d_attn, window, dilation):
    n_seq, seq_len, _ = qkv.shape
    half_window = (window // 2) // dilation
    tq = 2 * half_window
    tk = min(2 * tq, seq_len)
    lq = min(seq_len, 1024)
    bb = max(1, min(n_seq, 1024 // seq_len))
    banded = tk < seq_len
    n_var = 3 if banded else 1
    n_pairs = d_attn // LANES
    bsz = n_seq // dilation
    seq = seq_len * dilation
    if dilation == 1:
        assert bb == 1
        out_rows = lq

        def out_index(b, t):
            return (b, 0, t, 0)
    else:
        assert lq == seq_len and dilation % bb == 0
        out_rows = seq

        def out_index(b, t):
            return ((b * bb) // dilation, 0, 0, 0)

    kern = functools.partial(_attn_kernel, seq_len=seq_len, dilation=dilation, half_window=half_window,
                             tq=tq, tk=tk, bb=bb, width=d_attn, unroll=8 if banded else 4, mxu_rowsum=banded)
    if lq == seq_len and qkv.shape[2] == 3 * d_attn:
        in_specs = [pl.BlockSpec((bb, seq_len, 3 * d_attn), lambda b, t: (b, 0, 0))]
    else:
        in_specs = [pl.BlockSpec((bb, lq, d_attn), lambda b, t: (b, t, q_block)),
                    pl.BlockSpec((bb, seq_len, d_attn), lambda b, t: (b, 0, q_block + 1)),
                    pl.BlockSpec((bb, seq_len, d_attn), lambda b, t: (b, 0, q_block + 2))]
    vmem = (2 * bb * (lq + 2 * seq_len) * d_attn * 2 + 2 * out_rows * (d_attn + 2 * LANES) * 4
            + n_var * n_pairs * 2 * tq * tk * 4 + (12 << 20))
    return pl.pallas_call(
        kern,
        grid=(n_seq // bb, seq_len // lq),
        in_specs=[pl.BlockSpec(memory_space=pltpu.SMEM)] + in_specs,
        out_specs=pl.BlockSpec((None, n_pairs + 2, out_rows, LANES), out_index),
        out_shape=jax.ShapeDtypeStruct((bsz, n_pairs + 2, seq, LANES), F32),
        scratch_shapes=[pltpu.VMEM((n_var, n_pairs, 2 * tq, tk), F32)],
        compiler_params=_params(("arbitrary", "arbitrary"), vmem),
        name=f"dilated_attn_r{dilation}",
    )(slopes, *([qkv] * len(in_specs)))


def _mix_out_kernel(conv_ref, za_ref, before_ref, after_ref, cw_ref, cb_ref, gc_ref, ga_ref,
                    w_ref, x_ref, gate_ref, gp_ref, *rest, tiles_per_seq, n_chunks, n_branches):
    o_refs = rest[:n_branches]
    out_ref, expand_ref, y_ref, wn_ref = rest[n_branches:]
    tr, dc = za_ref.shape
    n_pairs = o_refs[0].shape[0] - 2
    u_cols, bg_cols, cg_cols, zc_cols = (slice(n * dc, (n + 1) * dc) for n in range(4))
    rc = tr // n_chunks
    i = pl.program_id(0)
    t_in_seq = i % tiles_per_seq

    @pl.when(i == 0)
    def _():
        k = lax.broadcasted_iota(jnp.int32, expand_ref.shape, 0)
        c = lax.broadcasted_iota(jnp.int32, expand_ref.shape, 1)
        expand_ref[...] = jnp.where(k % LANES == c // HEAD_DIM, 1.0, 0.0).astype(expand_ref.dtype)
        for r0 in range(0, w_ref.shape[0], 256):
            wn_ref[r0:r0 + 256, :] = w_ref[r0:r0 + 256, :].astype(wn_ref.dtype)

    ms = [ref[n_pairs] for ref in o_refs]
    ds = [ref[n_pairs + 1] for ref in o_refs]
    mmax = functools.reduce(jnp.maximum, ms)
    es = [jnp.exp2(m - mmax) for m in ms]
    inv = 1.0 / functools.reduce(jnp.add, [e * d for e, d in zip(es, ds)])
    weights = []
    for e in es:
        w = e * inv
        w_hi = w.astype(BF16)
        w_lo = (w - w_hi.astype(F32)).astype(BF16)
        weights.append(jnp.dot(jnp.concatenate([w_hi, w_lo], axis=1), expand_ref[...],
                               preferred_element_type=F32))

    last = BF16_SUBLANES - 1
    v_before = before_ref[last:last + 1, cg_cols].astype(F32) * before_ref[last:last + 1, u_cols].astype(F32)
    v_after = after_ref[0:1, cg_cols].astype(F32) * after_ref[0:1, u_cols].astype(F32)
    v_before = jnp.where(t_in_seq == 0, 0.0, v_before)
    v_after = jnp.where(t_in_seq == tiles_per_seq - 1, 0.0, v_after)
    vs = [conv_ref[k * rc:(k + 1) * rc, cg_cols].astype(F32) * conv_ref[k * rc:(k + 1) * rc, u_cols].astype(F32)
          for k in range(n_chunks)]
    row = lax.broadcasted_iota(jnp.int32, (rc, 1), 0)
    cw = cw_ref[...]

    for k in range(n_chunks):
        rows = slice(k * rc, (k + 1) * rc)
        above = v_before if k == 0 else vs[k - 1][rc - 1:rc, :]
        below = v_after if k == n_chunks - 1 else vs[k + 1][0:1, :]
        v_prev = jnp.where(row == 0, above, pltpu.roll(vs[k], 1, axis=0))
        v_next = jnp.where(row == rc - 1, below, pltpu.roll(vs[k], rc - 1, axis=0))
        conv = cw[0:1, :] * v_prev + cw[1:2, :] * vs[k] + cw[2:3, :] * v_next + cb_ref[...]
        yc = conv_ref[rows, bg_cols].astype(F32) * conv
        yc = yc * lax.rsqrt(jnp.mean(yc * yc, axis=-1, keepdims=True) + EPS) * gc_ref[...]
        y_ref[rows, 0:dc] = (yc * _silu(conv_ref[rows, zc_cols].astype(F32))).astype(y_ref.dtype)

        branch_out = [jnp.concatenate([ref[p, rows, :] for p in range(n_pairs)], axis=1) for ref in o_refs]
        om = functools.reduce(jnp.add, [wb[rows, :] * o for wb, o in zip(weights, branch_out)])
        ya = om * lax.rsqrt(jnp.mean(om * om, axis=-1, keepdims=True) + EPS) * ga_ref[...]
        y_ref[rows, dc:] = (ya * _silu(za_ref[rows, :].astype(F32))).astype(y_ref.dtype)

        y = jnp.dot(y_ref[rows, :], wn_ref[...], preferred_element_type=F32)
        yn = y * lax.rsqrt(jnp.mean(y * y, axis=-1, keepdims=True) + EPS) * gp_ref[...]
        out_ref[rows, :] = x_ref[rows, :] + gate_ref[...] * yn


def _mix_out(proj, outs, conv_w, conv_b, g_conv, g_attn, w_out, x2d, mod3, g_post, seq, d_conv, d_attn):
    m, d = x2d.shape
    tr = 256
    tps = seq // tr
    halo = BF16_SUBLANES
    hb = tr // halo
    n_halo_blocks = m // halo
    n_pairs = d_attn // LANES
    za_c = (4 * d_conv + 3 * d_attn) // d_attn

    def row_vec(n, rows=1):
        return pl.BlockSpec((rows, n), lambda i: (0, 0))

    branch = pl.BlockSpec((None, n_pairs + 2, tr, LANES), lambda i: (i // tps, 0, i % tps, 0))

    nb = len(outs)
    kern = functools.partial(_mix_out_kernel, tiles_per_seq=tps, n_chunks=2, n_branches=nb)
    vmem = (2 * ((4 * tr + 2 * halo) * d_conv * 2 + tr * d_attn * 2 + nb * tr * (d_attn + 2 * LANES) * 4
                 + 2 * tr * d * 4)
            + (d_conv + d_attn) * d * (4 + 2) + 2 * LANES * d_attn * 2 + tr * (d_conv + d_attn) * 2 + (16 << 20))
    return pl.pallas_call(
        kern,
        grid=(m // tr,),
        in_specs=[pl.BlockSpec((tr, 4 * d_conv), lambda i: (i, 0)),
                  pl.BlockSpec((tr, d_attn), lambda i: (i, za_c)),
                  pl.BlockSpec((halo, 4 * d_conv), lambda i: (jnp.maximum(i * hb - 1, 0), 0)),
                  pl.BlockSpec((halo, 4 * d_conv), lambda i: (jnp.minimum((i + 1) * hb, n_halo_blocks - 1), 0)),
                  row_vec(d_conv, CONV_WIDTH), row_vec(d_conv), row_vec(d_conv), row_vec(d_attn),
                  pl.BlockSpec((d_conv + d_attn, d), lambda i: (0, 0), pipeline_mode=pl.Buffered(1)),
                  pl.BlockSpec((tr, d), lambda i: (i, 0)),
                  pl.BlockSpec((None, 1, d), lambda i: (i // tps, 0, 2)),
                  row_vec(d)]
        + [branch] * nb,
        out_specs=pl.BlockSpec((tr, d), lambda i: (i, 0)),
        out_shape=jax.ShapeDtypeStruct((m, d), F32),
        scratch_shapes=[pltpu.VMEM((2 * LANES, d_attn), BF16), pltpu.VMEM((tr, d_conv + d_attn), BF16),
                        pltpu.VMEM((d_conv + d_attn, d), BF16)],
        compiler_params=_params(("arbitrary",), vmem),
        name="mix_out_proj",
    )(proj, proj, proj, proj,
      conv_w, conv_b.reshape(1, d_conv), g_conv.reshape(1, d_conv), g_attn.reshape(1, d_attn),
      w_out, x2d, mod3, g_post.reshape(1, d), *outs)


def _layer(x, mod, g_pre, w_in, conv_w, conv_b, g_conv, g_attn, w_out, g_post):
    bsz, seq, d = x.shape
    d_conv = conv_w.shape[1]
    d_attn = g_attn.shape[0]
    n_proj = w_in.shape[1]
    n_heads = d_attn // HEAD_DIM
    q_col = 4 * d_conv
    assert d_attn % LANES == 0 and d_conv == d_attn and n_proj == 4 * d_conv + 4 * d_attn
    windows = {r: w for w, r in DILATED_BRANCHES}
    dilations = tuple(r for _, r in DILATED_BRANCHES if r > 1)
    assert set(windows) == {1, *dilations}

    x2d = x.reshape(bsz * seq, d)
    mod3 = mod.reshape(bsz, 1, 3 * d)
    proj, *qkv_d = _in_projection(x2d, mod3, g_pre, w_in.astype(BF16), bsz, seq, q_col, d_attn, dilations)

    slopes = 2.0 ** (-8.0 * jnp.arange(1, n_heads + 1, dtype=F32) / n_heads)
    outs = [_attention_branch(proj.reshape(bsz, seq, n_proj), slopes, q_col // d_attn, d_attn, windows[1], 1)]
    for r, qkv in zip(dilations, qkv_d):
        outs.append(_attention_branch(qkv.reshape(bsz * r, seq // r, 3 * d_attn), slopes, 0, d_attn, windows[r], r))

    out = _mix_out(proj, outs, conv_w, conv_b, g_conv, g_attn, w_out, x2d, mod3, g_post, seq, d_conv, d_attn)
    return out.reshape(bsz, seq, d)


def kernel(x, c, w_ada, b_ada, g_pre, w_in, conv_w, conv_b, g_conv, g_attn, w_out, g_post):
    for layer in range(w_ada.shape[0]):
        mod = _modulation(c, w_ada[layer], b_ada[layer])
        x = _layer(x, mod, g_pre[layer], w_in[layer], conv_w[layer], conv_b[layer],
                   g_conv[layer], g_attn[layer], w_out[layer], g_post[layer])
    return x
```

```python
import functools

import jax
import jax.numpy as jnp
from jax import lax
from jax.experimental import pallas as pl
from jax.experimental.pallas import tpu as pltpu

HEAD_DIM = 64
CONV_WIDTH = 3
DILATED_BRANCHES = ((128, 1), (512, 4), (2048, 16))
EPS = 1e-6
NEG_INF = -1e30
LOG2_E = 1.4426950408889634

LANES = 128
BF16_SUBLANES = 16
V7X_VMEM_LIMIT = 60 * 1024 * 1024

F32 = jnp.float32
BF16 = jnp.bfloat16


def _silu(v):
    half = 0.5 * v
    return half + half * jnp.tanh(half)


def _params(semantics, vmem_bytes, flags=None):
    return pltpu.CompilerParams(dimension_semantics=semantics, flags=flags,
                                vmem_limit_bytes=min(int(vmem_bytes), V7X_VMEM_LIMIT))


def _mod_kernel(c_ref, w_ref, b_ref, o_ref):
    c_act = _silu(c_ref[...]).astype(BF16)
    w = w_ref[...].astype(BF16)
    o_ref[...] = jnp.dot(c_act, w, preferred_element_type=F32) + b_ref[...]


def _modulation(c, w_ada, b_ada):
    bsz, d = c.shape
    n = w_ada.shape[1]
    tn = 512
    return pl.pallas_call(
        _mod_kernel,
        grid=(n // tn,),
        in_specs=[pl.BlockSpec((bsz, d), lambda j: (0, 0)),
                  pl.BlockSpec((d, tn), lambda j: (0, j)),
                  pl.BlockSpec((1, tn), lambda j: (0, j))],
        out_specs=pl.BlockSpec((bsz, tn), lambda j: (0, j)),
        out_shape=jax.ShapeDtypeStruct((bsz, n), F32),
        compiler_params=_params(("arbitrary",), 4 * d * tn * 4),
        name="adaln_mod",
    )(c, w_ada, b_ada.reshape(1, n))


def _inproj_kernel(x0_ref, x1_ref, xb_ref, xa_ref, shift_ref, scale_ref, g_ref, w_ref, cw_ref, cb_ref, gc_ref,
                   o_ref, *rest, dilations, q_block, q_scale, chunk, mm_chunk, conv_chunk, tiles_per_seq):
    res_refs = rest[:len(dilations)]
    h_ref, u_ref, yc_ref, *stage_refs = rest[len(dilations):]
    tm, tn = o_ref.shape
    halo = xb_ref.shape[0]
    n_chunks = tm // chunk
    n_slabs = tn // LANES
    i = pl.program_id(0)
    j = pl.program_id(1)
    is_qkv = (j >= q_block) & (j < q_block + 3)
    t_in_seq = i % tiles_per_seq

    def rows_of(ck, size=chunk):
        return slice(ck * size, (ck + 1) * size)

    edge = slice(tm, tm + 2 * halo)
    n_conv_chunks = tm // conv_chunk

    def conv_rows(ck):
        return rows_of(ck, conv_chunk)

    def with_edge(ck, size=conv_chunk):
        return slice(ck * size, (ck + 1) * size + (2 * halo if (ck + 1) * size == tm else 0))

    @pl.when(j == 0)
    def _():
        g = g_ref[...]
        one_plus_scale = 1.0 + scale_ref[...]
        shift = shift_ref[...]

        def normed(xc):
            ms = jnp.mean(xc * xc, axis=-1, keepdims=True)
            return (xc * lax.rsqrt(ms + EPS) * g * one_plus_scale + shift).astype(BF16)

        h_ref[edge, :] = normed(jnp.concatenate([xb_ref[...], xa_ref[...]], axis=0))
        for ck in range(tm // mm_chunk):
            rows, rows_e = rows_of(ck, mm_chunk), with_edge(ck, mm_chunk)
            h_ref[rows, :] = normed(jnp.concatenate([x0_ref[rows, :], x1_ref[rows, :]], axis=1))
            u_ref[rows_e, :] = jnp.dot(h_ref[rows_e, :], w_ref[...], preferred_element_type=F32)

    @pl.when(j == 1)
    def _():
        for ck in range(n_conv_chunks):
            cg = jnp.dot(h_ref[with_edge(ck), :], w_ref[...], preferred_element_type=F32)
            u_ref[with_edge(ck), :] = cg * u_ref[with_edge(ck), :]
        before, after = slice(tm, tm + halo), slice(tm + halo, tm + 2 * halo)
        u_ref[before, :] = jnp.where(t_in_seq == 0, 0.0, u_ref[before, :])
        u_ref[after, :] = jnp.where(t_in_seq == tiles_per_seq - 1, 0.0, u_ref[after, :])

    @pl.when(j == 2)
    def _():
        cw = cw_ref[...]
        row = lax.broadcasted_iota(jnp.int32, (conv_chunk, 1), 0)
        for ck in range(n_conv_chunks):
            r0 = ck * conv_chunk
            v = u_ref[conv_rows(ck), :]
            above = u_ref[tm + halo - 1:tm + halo, :] if ck == 0 else u_ref[r0 - 1:r0, :]
            below = (u_ref[tm + halo:tm + halo + 1, :] if ck == n_conv_chunks - 1
                     else u_ref[r0 + conv_chunk:r0 + conv_chunk + 1, :])
            v_prev = jnp.where(row == 0, above, pltpu.roll(v, 1, axis=0))
            v_next = jnp.where(row == conv_chunk - 1, below, pltpu.roll(v, conv_chunk - 1, axis=0))
            yc_ref[conv_rows(ck), :] = cw[0:1, :] * v_prev + cw[1:2, :] * v + cw[2:3, :] * v_next + cb_ref[...]
            bg = jnp.dot(h_ref[conv_rows(ck), :], w_ref[...], preferred_element_type=F32)
            yc_ref[conv_rows(ck), :] = bg * yc_ref[conv_rows(ck), :]

    @pl.when(j == 3)
    def _():
        for ck in range(n_conv_chunks):
            yc = yc_ref[conv_rows(ck), :]
            yc_ref[conv_rows(ck), :] = yc * lax.rsqrt(jnp.mean(yc * yc, axis=-1, keepdims=True) + EPS) * gc_ref[...]
            zc = jnp.dot(h_ref[conv_rows(ck), :], w_ref[...], preferred_element_type=F32)
            o_ref[conv_rows(ck), :] = (yc_ref[conv_rows(ck), :] * _silu(zc)).astype(o_ref.dtype)

    @pl.when(j > q_block + 2)
    def _():
        acc = jnp.dot(h_ref[0:tm, :], w_ref[...], preferred_element_type=F32)
        o_ref[...] = acc.astype(o_ref.dtype)

    @pl.when(is_qkv)
    def _():
        fac = jnp.where(j == q_block, q_scale, 1.0).astype(F32)
        per_dot = mm_chunk // chunk
        for ck in range(n_chunks):
            if ck % per_dot == 0:
                big_rows = rows_of(ck // per_dot, mm_chunk)
                big = jnp.dot(h_ref[big_rows, :], w_ref[...], preferred_element_type=F32) * fac
                o_ref[big_rows, :] = big.astype(o_ref.dtype)
            acc = big[(ck % per_dot) * chunk:(ck % per_dot + 1) * chunk]
            slot = ck % 2
            for c in range(n_slabs):
                stage_refs[0][slot, c] = acc[:, c * LANES:(c + 1) * LANES]
            for c in range(n_slabs):
                lanes = slice(c * LANES, (c + 1) * LANES)
                classes = {0: (stage_refs[0].at[slot, c], chunk)}
                r_prev = 1
                for lvl, (r, res_ref) in enumerate(zip(dilations, res_refs)):
                    step = r // r_prev
                    new_classes = {}
                    for res_prev, (src, n_rows) in classes.items():
                        for k in range(step):
                            res = res_prev + r_prev * k
                            n_out = n_rows // step
                            rows = src[pl.ds(k, n_out, stride=step), :]
                            res_ref[res, ck * n_out:(ck + 1) * n_out, lanes] = rows.astype(res_ref.dtype)
                            if lvl + 1 < len(dilations):
                                dst = stage_refs[lvl + 1].at[slot, c, res]
                                dst[...] = rows
                                new_classes[res] = (dst, n_out)
                    classes, r_prev = new_classes, r


def _in_projection(x2d, mod3, g_pre, w_in_bf16, conv_w, conv_b, g_conv, bsz, seq, q_col, d_attn, dilations):
    m, d = x2d.shape
    n = w_in_bf16.shape[1]
    tm, tn = 1024, d_attn
    tps = seq // tm
    q_block = q_col // tn
    chunk = 256
    halo = 8
    hb = tm // halo
    assert q_block == 4 and n == (q_block + 4) * tn
    assert all(b % a == 0 for a, b in zip((1,) + dilations, dilations))
    kern = functools.partial(_inproj_kernel, dilations=dilations, q_block=q_block,
                             q_scale=HEAD_DIM ** -0.5 * LOG2_E, chunk=chunk, mm_chunk=512, conv_chunk=1024,
                             tiles_per_seq=tps)
    stage_shapes = [pltpu.VMEM((2, tn // LANES, chunk, LANES), F32)]
    stage_shapes += [pltpu.VMEM((2, tn // LANES, r, chunk // r, LANES), F32) for r in dilations[:-1]]

    def res_spec(r):
        return pl.BlockSpec((None, r, tm // r, tn),
                            lambda i, j: (i // tps, 0, i % tps, jnp.clip(j - q_block, 0, 2)))

    def row_vec(width, rows=1):
        return pl.BlockSpec((rows, width), lambda i, j: (0, 0))

    vmem = (2 * tm * d * 4 + 2 * d * tn * 2 + 2 * (1 + len(dilations)) * tm * tn * 2
            + tm * d * 2 + len(dilations) * 2 * chunk * tn * 4 + 3 * tm * tn * 4 + (10 << 20))
    return pl.pallas_call(
        kern,
        grid=(m // tm, n // tn),
        in_specs=[pl.BlockSpec((tm, d // 2), lambda i, j: (jnp.minimum(i + (j >= 1), m // tm - 1), 0)),
                  pl.BlockSpec((tm, d // 2), lambda i, j: (jnp.minimum(i + (j >= 2), m // tm - 1), 1)),
                  pl.BlockSpec((halo, d), lambda i, j: (jnp.maximum(i * hb - 1, 0), 0)),
                  pl.BlockSpec((halo, d), lambda i, j: (jnp.minimum((i + 1) * hb, m // halo - 1), 0)),
                  pl.BlockSpec((None, 1, d), lambda i, j: (i // tps, 0, 0)),
                  pl.BlockSpec((None, 1, d), lambda i, j: (i // tps, 0, 1)),
                  row_vec(d),
                  pl.BlockSpec((d, tn), lambda i, j: (0, jnp.where((j == 1) | (j == 2), 3 - j, j))),
                  row_vec(tn, CONV_WIDTH), row_vec(tn), row_vec(tn)],
        out_specs=[pl.BlockSpec((tm, tn), lambda i, j: (i, jnp.maximum(j - (q_block - 1), 0)))]
        + [res_spec(r) for r in dilations],
        out_shape=[jax.ShapeDtypeStruct((m, n - (q_block - 1) * tn), BF16)]
        + [jax.ShapeDtypeStruct((bsz, r, seq // r, 3 * d_attn), BF16) for r in dilations],
        scratch_shapes=[pltpu.VMEM((tm + 2 * halo, d), BF16), pltpu.VMEM((tm + 2 * halo, tn), F32),
                        pltpu.VMEM((tm, tn), F32)] + stage_shapes,
        compiler_params=_params(("arbitrary", "arbitrary"), vmem),
        name="in_proj",
    )(x2d, x2d, x2d, x2d, mod3, mod3, g_pre.reshape(1, d), w_in_bf16,
      conv_w, conv_b.reshape(1, tn), g_conv.reshape(1, tn))


def _attn_kernel(slopes_ref, *refs, seq_len, dilation, half_window, tq, tk, bb, width, unroll, mxu_rowsum,
                 keys_on_rows, cast_weight):
    if cast_weight:
        *in_refs, w_ref, o_ref, wo_ref, bias_ref = refs
        wo_ref[...] = w_ref[...].astype(wo_ref.dtype)
    else:
        *in_refs, o_ref, bias_ref = refs
    if len(in_refs) == 1:
        q_ref = k_ref = v_ref = in_refs[0]
        k_col, v_col = width, 2 * width
    else:
        q_ref, k_ref, v_ref = in_refs
        k_col = v_col = 0
    lq = q_ref.shape[1]
    n_pairs = width // LANES
    n_qt = lq // tq
    n_var = bias_ref.shape[0]
    max_ks = seq_len - tk
    first = (pl.program_id(0) == 0) & (pl.program_id(1) == 0)

    @pl.when(first)
    def _():
        qi = lax.broadcasted_iota(jnp.int32, (tq, tk), 0)
        ki = lax.broadcasted_iota(jnp.int32, (tq, tk), 1)
        for var in range(n_var):
            delta = (0, half_window, tk - tq)[var]
            off = jnp.abs(ki - qi - delta)
            valid = off <= half_window
            dist = (dilation * off).astype(F32)
            for h in range(2 * n_pairs):
                bias = jnp.where(valid, (-LOG2_E * slopes_ref[h]) * dist, NEG_INF)
                r0 = (h % 2) * tq
                if keys_on_rows:
                    bias_ref[var, h // 2, :, r0:r0 + tq] = bias
                else:
                    bias_ref[var, h // 2, r0:r0 + tq, :] = bias

    lane = lax.broadcasted_iota(jnp.int32, (1, LANES), 1)
    even_head = lane < HEAD_DIM
    keep_even = even_head.astype(BF16)
    keep_odd = 1.0 - keep_even
    ones = jnp.ones((tk, LANES), BF16)
    q_tile0 = pl.program_id(1) * n_qt

    def tile(b, jt):
        qs_local = jt * tq if isinstance(jt, int) else pl.multiple_of(jt * tq, tq)
        qs = (q_tile0 + jt) * tq
        ks = pl.multiple_of(jnp.clip(qs - half_window, 0, max_ks), half_window)
        if n_var == 1:
            var = 0
        else:
            var = jnp.where(qs == 0, 0, jnp.where(qs == seq_len - tq, 2, 1))
        if dilation == 1:
            out_rows = pl.ds(qs_local, tq)
        else:
            res = (pl.program_id(0) * bb + b) % dilation
            out_rows = pl.ds(res + dilation * qs, tq, stride=dilation)
        m_tile = jnp.zeros((tq, LANES), F32)
        d_tile = jnp.ones((tq, LANES), F32)
        m_rows, d_rows = [], []
        for p in range(n_pairs):
            cols = slice(p * LANES, (p + 1) * LANES)
            q2 = q_ref[b, pl.ds(qs_local, tq), cols]
            kw = k_ref[b, pl.ds(ks, tk), k_col + p * LANES:k_col + (p + 1) * LANES]
            vw = v_ref[b, pl.ds(ks, tk), v_col + p * LANES:v_col + (p + 1) * LANES]
            qq = jnp.concatenate([q2 * keep_even, q2 * keep_odd], axis=0)
            if keys_on_rows:
                st = lax.dot_general(kw, qq, (((1,), (1,)), ((), ())), preferred_element_type=F32)
                st = st + bias_ref[var, p]
                m = jnp.max(st, axis=0, keepdims=True)
                e = jnp.exp2(st - m)
                den = jnp.sum(e, axis=0, keepdims=True)
                pv = lax.dot_general(e.astype(BF16), vw, (((0,), (0,)), ((), ())), preferred_element_type=F32)
                o_ref[p, out_rows, :] = jnp.where(even_head, pv[:tq], pv[tq:])
                m_rows += [m[:, :tq], m[:, tq:]]
                d_rows += [den[:, :tq], den[:, tq:]]
                continue
            s = lax.dot_general(qq, kw, (((1,), (1,)), ((), ())), preferred_element_type=F32)
            s = s + bias_ref[var, p]
            m = jnp.max(s, axis=-1, keepdims=True)
            e = jnp.exp2(s - m)
            if mxu_rowsum:
                pv = jnp.dot(e.astype(BF16), jnp.concatenate([vw, ones], axis=1), preferred_element_type=F32)
                pv, den = pv[:, :LANES], pv[:, LANES:]
            else:
                den = jnp.sum(e, axis=-1, keepdims=True)
                pv = jnp.dot(e.astype(BF16), vw, preferred_element_type=F32)
            o_ref[p, out_rows, :] = jnp.where(even_head, pv[:tq], pv[tq:])
            m_tile = jnp.where(lane == 2 * p, m[:tq], jnp.where(lane == 2 * p + 1, m[tq:], m_tile))
            d_tile = jnp.where(lane == 2 * p, den[:tq], jnp.where(lane == 2 * p + 1, den[tq:], d_tile))
        if keys_on_rows:
            pad = jnp.zeros((LANES - 2 * n_pairs, tq), F32)
            m_tile = jnp.concatenate(m_rows + [pad], axis=0).T
            d_tile = jnp.concatenate(d_rows + [pad + 1.0], axis=0).T
        o_ref[n_pairs, out_rows, :] = m_tile
        o_ref[n_pairs + 1, out_rows, :] = d_tile

    n_tiles = bb * n_qt
    group = min(unroll, n_tiles)

    def tile_group(it, carry):
        for u in range(group):
            idx = it * group + u
            tile(*((idx, 0) if n_qt == 1 else (idx // n_qt, idx % n_qt)))
        return carry

    if n_tiles == group:
        tile_group(0, 0)
    else:
        lax.fori_loop(0, n_tiles // group, tile_group, 0)


def _attention_branch(qkv, slopes, q_block, d_attn, window, dilation, cast=None):
    n_seq, seq_len, _ = qkv.shape
    half_window = (window // 2) // dilation
    tq = 2 * half_window
    tk = min(2 * tq, seq_len)
    lq = min(seq_len, 1024)
    bb = max(1, min(n_seq, 1024 // seq_len))
    banded = tk < seq_len
    keys_on_rows = not banded and tq == tk == LANES
    n_var = 3 if banded else 1
    n_pairs = d_attn // LANES
    bsz = n_seq // dilation
    seq = seq_len * dilation
    if dilation == 1:
        assert bb == 1
        out_rows = lq

        def out_index(b, t):
            return (b, 0, t, 0)
    else:
        assert lq == seq_len and dilation % bb == 0
        out_rows = seq

        def out_index(b, t):
            return ((b * bb) // dilation, 0, 0, 0)

    kern = functools.partial(_attn_kernel, seq_len=seq_len, dilation=dilation, half_window=half_window,
                             tq=tq, tk=tk, bb=bb, width=d_attn, unroll=8, mxu_rowsum=banded,
                             keys_on_rows=keys_on_rows, cast_weight=cast is not None)
    if lq == seq_len and qkv.shape[2] == 3 * d_attn:
        in_specs = [pl.BlockSpec((bb, seq_len, 3 * d_attn), lambda b, t: (b, 0, 0))]
    else:
        in_specs = [pl.BlockSpec((bb, lq, d_attn), lambda b, t: (b, t, q_block)),
                    pl.BlockSpec((bb, seq_len, d_attn), lambda b, t: (b, 0, q_block + 1)),
                    pl.BlockSpec((bb, seq_len, d_attn), lambda b, t: (b, 0, q_block + 2))]
    vmem = (2 * bb * (lq + 2 * seq_len) * d_attn * 2 + 2 * out_rows * (d_attn + 2 * LANES) * 4
            + n_var * n_pairs * 2 * tq * tk * 4 + (12 << 20))
    grid = (n_seq // bb, seq_len // lq)
    operands = [qkv] * len(in_specs)
    out_specs = pl.BlockSpec((None, n_pairs + 2, out_rows, LANES), out_index)
    out_shape = jax.ShapeDtypeStruct((bsz, n_pairs + 2, seq, LANES), F32)
    if cast is not None:
        rows, cols = cast.shape[0] // (grid[0] * grid[1]), cast.shape[1]
        assert rows * grid[0] * grid[1] == cast.shape[0] and rows % BF16_SUBLANES == 0
        w_spec = pl.BlockSpec((rows, cols), lambda b, t: (b * grid[1] + t, 0))
        in_specs, operands = in_specs + [w_spec], operands + [cast]
        out_specs, out_shape = [out_specs, w_spec], [out_shape, jax.ShapeDtypeStruct(cast.shape, BF16)]
        vmem += 2 * rows * cols * (4 + 2)
    return pl.pallas_call(
        kern,
        grid=grid,
        in_specs=[pl.BlockSpec(memory_space=pltpu.SMEM)] + in_specs,
        out_specs=out_specs,
        out_shape=out_shape,
        scratch_shapes=[pltpu.VMEM((n_var, n_pairs, tk, 2 * tq) if keys_on_rows else (n_var, n_pairs, 2 * tq, tk),
                                   F32)],
        compiler_params=_params(("arbitrary", "arbitrary"), vmem),
        name=f"dilated_attn_r{dilation}",
    )(slopes, *operands)


def _mix_out_kernel(yc_ref, za_ref, ga_ref, w_ref, x_ref, gate_ref, gp_ref, *rest, n_chunks, n_branches):
    o_refs = rest[:n_branches]
    out_ref, expand_ref, y_ref, wn_ref = rest[n_branches:]
    tr, dc = za_ref.shape
    n_pairs = o_refs[0].shape[0] - 2
    rc = tr // n_chunks
    i = pl.program_id(0)

    @pl.when(i == 0)
    def _():
        k = lax.broadcasted_iota(jnp.int32, expand_ref.shape, 0)
        c = lax.broadcasted_iota(jnp.int32, expand_ref.shape, 1)
        expand_ref[...] = jnp.where(k % LANES == c // HEAD_DIM, 1.0, 0.0).astype(expand_ref.dtype)
        for r0 in range(0, w_ref.shape[0], 256):
            wn_ref[r0:r0 + 256, :] = w_ref[r0:r0 + 256, :].astype(wn_ref.dtype)

    ms = [ref[n_pairs] for ref in o_refs]
    ds = [ref[n_pairs + 1] for ref in o_refs]
    mmax = functools.reduce(jnp.maximum, ms)
    es = [jnp.exp2(m - mmax) for m in ms]
    inv = 1.0 / functools.reduce(jnp.add, [e * d for e, d in zip(es, ds)])
    split = []
    for e in es:
        w = e * inv
        w_hi = w.astype(BF16)
        w_lo = (w - w_hi.astype(F32)).astype(BF16)
        split.append(jnp.concatenate([w_hi, w_lo], axis=1))
    wide = jnp.dot(jnp.concatenate(split, axis=0), expand_ref[...], preferred_element_type=F32)
    weights = [wide[b * tr:(b + 1) * tr] for b in range(n_branches)]

    gain_gate = gp_ref[...] * gate_ref[...]

    for k in range(n_chunks):
        rows = slice(k * rc, (k + 1) * rc)
        y_ref[rows, 0:dc] = yc_ref[rows, :]

        branch_out = [jnp.concatenate([ref[p, rows, :] for p in range(n_pairs)], axis=1) for ref in o_refs]
        om = functools.reduce(jnp.add, [wb[rows, :] * o for wb, o in zip(weights, branch_out)])
        ya = om * lax.rsqrt(jnp.mean(om * om, axis=-1, keepdims=True) + EPS) * ga_ref[...]
        y_ref[rows, dc:] = (ya * _silu(za_ref[rows, :].astype(F32))).astype(y_ref.dtype)

        y = jnp.dot(y_ref[rows, :], wn_ref[...], preferred_element_type=F32)
        yn = y * lax.rsqrt(jnp.mean(y * y, axis=-1, keepdims=True) + EPS)
        out_ref[rows, :] = x_ref[rows, :] + yn * gain_gate


def _mix_out(proj, za_block, outs, g_attn, w_out, x2d, mod3, g_post, seq, d_conv, d_attn):
    m, d = x2d.shape
    tr = 512
    tps = seq // tr
    n_pairs = d_attn // LANES

    def row_vec(n):
        return pl.BlockSpec((1, n), lambda i: (0, 0))

    branch = pl.BlockSpec((None, n_pairs + 2, tr, LANES), lambda i: (i // tps, 0, i % tps, 0))

    nb = len(outs)
    kern = functools.partial(_mix_out_kernel, n_chunks=1, n_branches=nb)
    vmem = (2 * (tr * d_conv * 2 + tr * d_attn * 2 + nb * tr * (d_attn + 2 * LANES) * 4
                 + 2 * tr * d * 4)
            + (d_conv + d_attn) * d * (2 + 2) + 2 * LANES * d_attn * 2 + tr * (d_conv + d_attn) * 2 + (16 << 20))
    return pl.pallas_call(
        kern,
        grid=(m // tr,),
        in_specs=[pl.BlockSpec((tr, d_conv), lambda i: (i, 0)),
                  pl.BlockSpec((tr, d_attn), lambda i: (i, za_block)),
                  row_vec(d_attn),
                  pl.BlockSpec((d_conv + d_attn, d), lambda i: (0, 0), pipeline_mode=pl.Buffered(1)),
                  pl.BlockSpec((tr, d), lambda i: (i, 0)),
                  pl.BlockSpec((None, 1, d), lambda i: (i // tps, 0, 2)),
                  row_vec(d)]
        + [branch] * nb,
        out_specs=pl.BlockSpec((tr, d), lambda i: (i, 0)),
        out_shape=jax.ShapeDtypeStruct((m, d), F32),
        scratch_shapes=[pltpu.VMEM((2 * LANES, d_attn), BF16), pltpu.VMEM((tr, d_conv + d_attn), BF16),
                        pltpu.VMEM((d_conv + d_attn, d), BF16)],
        compiler_params=_params(("arbitrary",), vmem),
        name="mix_out_proj",
    )(proj, proj, g_attn.reshape(1, d_attn), w_out, x2d, mod3, g_post.reshape(1, d), *outs)


def _layer(x, mod, g_pre, w_in, conv_w, conv_b, g_conv, g_attn, w_out, g_post):
    bsz, seq, d = x.shape
    d_conv = conv_w.shape[1]
    d_attn = g_attn.shape[0]
    n_proj = w_in.shape[1]
    n_heads = d_attn // HEAD_DIM
    q_col = 4 * d_conv
    assert d_attn % LANES == 0 and d_conv == d_attn and n_proj == 4 * d_conv + 4 * d_attn
    windows = {r: w for w, r in DILATED_BRANCHES}
    dilations = tuple(r for _, r in DILATED_BRANCHES if r > 1)
    assert set(windows) == {1, *dilations}

    x2d = x.reshape(bsz * seq, d)
    mod3 = mod.reshape(bsz, 1, 3 * d)
    proj, *qkv_d = _in_projection(x2d, mod3, g_pre, w_in.astype(BF16), conv_w, conv_b, g_conv,
                                  bsz, seq, q_col, d_attn, dilations)
    n_out = proj.shape[1]

    slopes = 2.0 ** (-8.0 * jnp.arange(1, n_heads + 1, dtype=F32) / n_heads)
    outs = [_attention_branch(proj.reshape(bsz, seq, n_out), slopes, 1, d_attn, windows[1], 1)]
    w_out_bf16 = None
    for r, qkv in zip(dilations, qkv_d):
        o = _attention_branch(qkv.reshape(bsz * r, seq // r, 3 * d_attn), slopes, 0, d_attn, windows[r], r,
                              cast=w_out if r == dilations[-1] else None)
        if r == dilations[-1]:
            o, w_out_bf16 = o
        outs.append(o)

    out = _mix_out(proj, 4, outs, g_attn, w_out_bf16, x2d, mod3, g_post, seq, d_conv, d_attn)
    return out.reshape(bsz, seq, d)


def kernel(x, c, w_ada, b_ada, g_pre, w_in, conv_w, conv_b, g_conv, g_attn, w_out, g_post):
    for layer in range(w_ada.shape[0]):
        mod = _modulation(c, w_ada[layer], b_ada[layer])
        x = _layer(x, mod, g_pre[layer], w_in[layer], conv_w[layer], conv_b[layer],
                   g_conv[layer], g_attn[layer], w_out[layer], g_post[layer])
    return x
```
